```python
import math
import jax, jax.numpy as jnp
from jax import lax
import numpy as np

D_MODEL = 1024
BATCH = 4
SEQ = 4096
DEPTH = 4
DEC_BATCH = 32
DEC_SEQ = 1
PAST_LEN = 8192
PAGE_SIZE = 128

POOL_WIDTH = 256
POOL_WINDOWS = (2, 4, 8, 16)
POOL_GROUPS = len(POOL_WINDOWS)
POOL_GROUP_DIM = POOL_WIDTH // POOL_GROUPS
POOL_BUF = max(POOL_WINDOWS) - 1
ATTN_HEADS = 6
HEAD_DIM = 64
ATTN_WIDTH = ATTN_HEADS * HEAD_DIM
DILATED = ((128, 1), (512, 4), (2048, 16))
ATTN_REACH = max(w for w, _ in DILATED)
ATTN_BLOCK = 128
CONV_WIDTH = D_MODEL - POOL_WIDTH - ATTN_WIDTH
CONV_KERNEL = 31
CONV_BUF = CONV_KERNEL - 1
MIX_WIDTH = POOL_WIDTH + ATTN_WIDTH + CONV_WIDTH
IN_PROJ_WIDTH = POOL_WIDTH + 3 * ATTN_WIDTH + 2 * CONV_WIDTH
N_EXPERTS = 32
TOP_K = 4
D_EXPERT = D_MODEL
SWIGLU_ALPHA = 1.702
SWIGLU_LIMIT = 7.0
MOE_BLOCK = 128
DEEPNORM_ALPHA = (2 * DEPTH) ** 0.25
DEEPNORM_BETA = (8 * DEPTH) ** -0.25
LN_EPS = 1e-5

kernel_name = "hymba_pool_dilated_conformer_moe_step"


def alibi_slopes(n):
    def pow2(k):
        start = 2.0 ** (-8.0 / k)
        return [start ** (i + 1) for i in range(k)]
    if math.log2(n).is_integer():
        s = pow2(n)
    else:
        c = 2 ** math.floor(math.log2(n))
        s = pow2(c) + pow2(2 * c)[0::2][: n - c]
    return jnp.asarray(s, jnp.float32)


def layer_norm(x, g, b):
    xf = x.astype(jnp.float32)
    mu = xf.mean(-1, keepdims=True)
    var = jnp.square(xf - mu).mean(-1, keepdims=True)
    return ((xf - mu) * lax.rsqrt(var + LN_EPS) * g + b).astype(x.dtype)


def pool_mix(u, prefix, pos0, pool_w, pool_scale):
    B, T, _ = u.shape
    ext = jnp.concatenate([prefix, u], axis=1)
    extf = ext.astype(jnp.float32)
    csum = jnp.concatenate([jnp.zeros((B, 1, POOL_WIDTH), jnp.float32), jnp.cumsum(extf, axis=1)], axis=1)
    end = csum[:, POOL_BUF + 1:]
    pos = pos0 + jnp.arange(T)
    diffs = []
    for g, w in enumerate(POOL_WINDOWS):
        sl = slice(g * POOL_GROUP_DIM, (g + 1) * POOL_GROUP_DIM)
        start = csum[:, POOL_BUF + 1 - w: POOL_BUF + 1 - w + T, sl]
        cnt = jnp.minimum(w, pos + 1).astype(jnp.float32)[None, :, None]
        diffs.append((end[..., sl] - start) / cnt - extf[:, POOL_BUF:, sl])
    d = jnp.stack(diffs, axis=2)
    y = jnp.einsum('btgc,gcd->btgd', d, pool_w).reshape(B, T, POOL_WIDTH) * pool_scale
    return y.astype(u.dtype), ext[:, -POOL_BUF:]


def conv_mix(c_in, prefix, conv_w, conv_b, ln_g, ln_b):
    a, gate = jnp.split(c_in, 2, axis=-1)
    u = a * jax.nn.sigmoid(gate)
    ext = jnp.concatenate([prefix, u], axis=1)
    y = lax.conv_general_dilated(ext, conv_w[:, None, :], window_strides=(1,), padding='VALID',
                                 dimension_numbers=('NWC', 'WIO', 'NWC'),
                                 feature_group_count=CONV_WIDTH) + conv_b
    y = jax.nn.silu(layer_norm(y, ln_g, ln_b))
    return y, ext[:, -CONV_BUF:]


def dilated_band_branch(q, k, v, slopes, window, dilation):
    B, S, H, Dh = q.shape
    L = S // dilation
    Lp = -(-L // ATTN_BLOCK) * ATTN_BLOCK
    nb = Lp // ATTN_BLOCK
    n_back = window // dilation

    def to_blocks(x):
        x = x.reshape(B, L, dilation, H, Dh).transpose(0, 2, 1, 3, 4)
        x = jnp.pad(x, ((0, 0), (0, 0), (0, Lp - L), (0, 0), (0, 0)))
        return x.reshape(B, dilation, nb, ATTN_BLOCK, H, Dh)

    def with_prev(xb):
        prev = jnp.pad(xb, ((0, 0), (0, 0), (1, 0), (0, 0), (0, 0), (0, 0)))[:, :, :-1]
        return jnp.concatenate([prev, xb], axis=3)

    qb = to_blocks(q)
    kb = with_prev(to_blocks(k))
    vb = with_prev(to_blocks(v))
    s = jnp.einsum('brnqhd,brnkhd->brnhqk', qb, kb, preferred_element_type=jnp.float32) * (HEAD_DIM ** -0.5)
    qi = jnp.arange(ATTN_BLOCK)[:, None] + ATTN_BLOCK
    ki = jnp.arange(2 * ATTN_BLOCK)[None, :]
    delta = qi - ki
    blk = jnp.arange(nb)[:, None, None]
    valid = (delta >= 0) & (delta <= n_back) & (blk * ATTN_BLOCK + ki - ATTN_BLOCK >= 0)
    bias = -slopes[:, None, None] * (delta * dilation).astype(jnp.float32)[None]
    s = jnp.where(valid[None, None, :, None], s + bias, -jnp.inf)
    m = s.max(-1)
    p = jnp.exp(s - m[..., None])
    l = p.sum(-1)
    acc = jnp.einsum('brnhqk,brnkhd->brnqhd', p, vb.astype(jnp.float32))

    def back(x):
        x = x.reshape((B, dilation, Lp) + x.shape[4:])[:, :, :L]
        x = jnp.swapaxes(x, 1, 2)
        return x.reshape((B, S) + x.shape[3:])

    return back(acc), back(jnp.swapaxes(m, 3, 4)), back(jnp.swapaxes(l, 3, 4))


def dilated_gather_branch(q, k_ext, v_ext, slopes, window, dilation, n_buf, pos0):
    T = q.shape[1]
    n_back = window // dilation
    j = jnp.arange(T)[:, None]
    kk = jnp.arange(n_back + 1)[None, :]
    idx = n_buf + j - kk * dilation
    valid = (idx >= 0) & (pos0 + j - kk * dilation >= 0)
    idx = jnp.maximum(idx, 0)
    kg = k_ext[:, idx]
    vg = v_ext[:, idx]
    s = jnp.einsum('bthd,btkhd->bthk', q, kg, preferred_element_type=jnp.float32) * (HEAD_DIM ** -0.5)
    bias = -slopes[:, None] * (kk * dilation).astype(jnp.float32)
    s = jnp.where(valid[None, :, None, :], s + bias[None, None], -jnp.inf)
    m = s.max(-1)
    p = jnp.exp(s - m[..., None])
    l = p.sum(-1)
    acc = jnp.einsum('bthk,btkhd->bthd', p, vg.astype(jnp.float32))
    return acc, m, l


def merge_dilations(branches):
    acc = jnp.stack([b[0] for b in branches])
    m = jnp.stack([b[1] for b in branches])
    l = jnp.stack([b[2] for b in branches])
    w = jnp.exp(m - m.max(0))
    denom = (w * l).sum(0)
    return (w[..., None] * acc).sum(0) / denom[..., None]


def moe_ffn(h, router_w, router_b, e_w_in, e_b_in, e_w_out, e_b_out):
    B, T, D = h.shape
    xt = h.reshape(-1, D)
    M = xt.shape[0]
    logits = jnp.einsum('md,de->me', xt, router_w, preferred_element_type=jnp.float32) + router_b
    top_v, top_i = lax.top_k(logits, TOP_K)
    gates = jax.nn.softmax(top_v, axis=-1)
    P = M * TOP_K
    e_flat = top_i.reshape(-1)
    t_flat = jnp.repeat(jnp.arange(M, dtype=jnp.int32), TOP_K)
    g_flat = gates.reshape(-1)
    order = jnp.argsort(e_flat)
    es, ts, gs = e_flat[order], t_flat[order], g_flat[order]
    counts = jnp.bincount(e_flat, length=N_EXPERTS)
    starts = jnp.cumsum(counts) - counts
    padded = (counts + MOE_BLOCK - 1) // MOE_BLOCK * MOE_BLOCK
    pad_end = jnp.cumsum(padded)
    pad_start = pad_end - padded
    dest = pad_start[es] + jnp.arange(P) - starts[es]
    n_blocks = -(-P // MOE_BLOCK) + N_EXPERTS
    n_slots = n_blocks * MOE_BLOCK
    slot_tok = jnp.full((n_slots,), M, jnp.int32).at[dest].set(ts)
    slot_gate = jnp.zeros((n_slots,), jnp.float32).at[dest].set(gs)
    block_exp = jnp.minimum(jnp.searchsorted(pad_end, jnp.arange(n_blocks) * MOE_BLOCK, side='right'), N_EXPERTS - 1)
    x_pad = jnp.concatenate([xt, jnp.zeros((1, D), xt.dtype)], axis=0)
    xb = x_pad[slot_tok].reshape(n_blocks, MOE_BLOCK, D)

    def run_block(args):
        xblk, e = args
        hh = xblk @ e_w_in[e] + e_b_in[e]
        gate, up = jnp.split(hh, 2, axis=-1)
        gate = jnp.minimum(gate, SWIGLU_LIMIT)
        up = jnp.clip(up, -SWIGLU_LIMIT, SWIGLU_LIMIT)
        act = gate * jax.nn.sigmoid(SWIGLU_ALPHA * gate) * (up + 1)
        return act @ e_w_out[e] + e_b_out[e]

    yb = lax.map(run_block, (xb, block_exp)).reshape(n_slots, D)
    y = jax.ops.segment_sum(yb.astype(jnp.float32) * slot_gate[:, None], slot_tok, num_segments=M + 1)[:M]
    return y.reshape(B, T, D).astype(h.dtype)


def decoder_layer(x, pool_prefix, conv_prefix, k_prefix, v_prefix, pos0,
                  w_in, pool_w, pool_scale, conv_w, conv_b, conv_ln_g, conv_ln_b, w_o,
                  ln1_g, ln1_b, router_w, router_b, e_w_in, e_b_in, e_w_out, e_b_out,
                  ln2_g, ln2_b, slopes):
    B, T, _ = x.shape
    proj = jnp.einsum('btd,de->bte', x, w_in)
    P, A = POOL_WIDTH, ATTN_WIDTH
    u_pool, q, k, v, c_in = jnp.split(proj, [P, P + A, P + 2 * A, P + 3 * A], axis=-1)
    q = q.reshape(B, T, ATTN_HEADS, HEAD_DIM)
    k = k.reshape(B, T, ATTN_HEADS, HEAD_DIM)
    v = v.reshape(B, T, ATTN_HEADS, HEAD_DIM)
    y_pool, pool_state = pool_mix(u_pool, pool_prefix, pos0, pool_w, pool_scale)
    if k_prefix is None:
        branches = [dilated_band_branch(q, k, v, slopes, w, d) for w, d in DILATED]
    else:
        n_buf = k_prefix.shape[1]
        k_ext = jnp.concatenate([k_prefix, k], axis=1)
        v_ext = jnp.concatenate([v_prefix, v], axis=1)
        branches = [dilated_gather_branch(q, k_ext, v_ext, slopes, w, d, n_buf, pos0) for w, d in DILATED]
    y_attn = merge_dilations(branches).reshape(B, T, ATTN_WIDTH).astype(x.dtype)
    y_conv, conv_state = conv_mix(c_in, conv_prefix, conv_w, conv_b, conv_ln_g, conv_ln_b)
    mix = jnp.concatenate([y_pool, y_attn, y_conv], axis=-1) @ w_o
    x = layer_norm(DEEPNORM_ALPHA * x + mix, ln1_g, ln1_b)
    x = layer_norm(DEEPNORM_ALPHA * x + moe_ffn(x, router_w, router_b, e_w_in, e_b_in, e_w_out, e_b_out), ln2_g, ln2_b)
    return x, pool_state, conv_state, k, v


def setup_inputs(seed: int = 0) -> dict:
    key = jax.random.key(seed)
    ks = jax.random.split(key, 24)
    f32 = jnp.float32
    n_buf = min(ATTN_REACH, PAST_LEN)
    nrm = lambda k, s: jax.random.normal(k, s, f32)
    return {
        "x_prompt": nrm(ks[0], (BATCH, SEQ, D_MODEL)),
        "x_sample": nrm(ks[1], (DEC_BATCH, DEC_SEQ, D_MODEL)),
        "state_pool": nrm(ks[2], (DEPTH, DEC_BATCH, POOL_BUF, POOL_WIDTH)),
        "state_conv": 0.5 * nrm(ks[3], (DEPTH, DEC_BATCH, CONV_BUF, CONV_WIDTH)),
        "cache_attn_k": nrm(ks[4], (DEPTH, DEC_BATCH, n_buf, ATTN_HEADS, HEAD_DIM)),
        "cache_attn_v": nrm(ks[5], (DEPTH, DEC_BATCH, n_buf, ATTN_HEADS, HEAD_DIM)),
        "w_in": nrm(ks[6], (DEPTH, D_MODEL, IN_PROJ_WIDTH)) * D_MODEL ** -0.5,
        "pool_w": nrm(ks[7], (DEPTH, POOL_GROUPS, POOL_GROUP_DIM, POOL_GROUP_DIM)) * POOL_GROUP_DIM ** -0.5,
        "pool_scale": 1.0 + 0.02 * nrm(ks[8], (DEPTH, POOL_WIDTH)),
        "conv_w": nrm(ks[9], (DEPTH, CONV_KERNEL, CONV_WIDTH)) * CONV_KERNEL ** -0.5,
        "conv_b": 0.01 * nrm(ks[10], (DEPTH, CONV_WIDTH)),
        "conv_ln_g": 1.0 + 0.02 * nrm(ks[11], (DEPTH, CONV_WIDTH)),
        "conv_ln_b": 0.01 * nrm(ks[12], (DEPTH, CONV_WIDTH)),
        "w_o": nrm(ks[13], (DEPTH, MIX_WIDTH, D_MODEL)) * MIX_WIDTH ** -0.5 * DEEPNORM_BETA,
        "ln1_g": 1.0 + 0.02 * nrm(ks[14], (DEPTH, D_MODEL)),
        "ln1_b": 0.01 * nrm(ks[15], (DEPTH, D_MODEL)),
        "router_w": nrm(ks[16], (DEPTH, D_MODEL, N_EXPERTS)) * D_MODEL ** -0.5,
        "router_b": 0.01 * nrm(ks[17], (DEPTH, N_EXPERTS)),
        "expert_w_in": nrm(ks[18], (DEPTH, N_EXPERTS, D_MODEL, 2 * D_EXPERT)) * D_MODEL ** -0.5,
        "expert_b_in": 0.01 * nrm(ks[19], (DEPTH, N_EXPERTS, 2 * D_EXPERT)),
        "expert_w_out": nrm(ks[20], (DEPTH, N_EXPERTS, D_EXPERT, D_MODEL)) * D_EXPERT ** -0.5 * DEEPNORM_BETA,
        "expert_b_out": 0.01 * nrm(ks[21], (DEPTH, N_EXPERTS, D_MODEL)),
        "ln2_g": 1.0 + 0.02 * nrm(ks[22], (DEPTH, D_MODEL)),
        "ln2_b": 0.01 * nrm(ks[23], (DEPTH, D_MODEL)),
    }


def reference(x_prompt, x_sample, state_pool, state_conv, cache_attn_k, cache_attn_v,
              w_in, pool_w, pool_scale, conv_w, conv_b, conv_ln_g, conv_ln_b, w_o,
              ln1_g, ln1_b, router_w, router_b, expert_w_in, expert_b_in, expert_w_out,
              expert_b_out, ln2_g, ln2_b):
    slopes = alibi_slopes(ATTN_HEADS)
    y_p, y_s = x_prompt, x_sample
    Bp, Sp, _ = x_prompt.shape
    keep = min(ATTN_REACH, Sp)
    p_pool, p_conv, p_k, p_v = [], [], [], []
    s_pool, s_conv, s_k, s_v = [], [], [], []
    for l in range(DEPTH):
        lw = (w_in[l], pool_w[l], pool_scale[l], conv_w[l], conv_b[l], conv_ln_g[l], conv_ln_b[l], w_o[l],
              ln1_g[l], ln1_b[l], router_w[l], router_b[l], expert_w_in[l], expert_b_in[l],
              expert_w_out[l], expert_b_out[l], ln2_g[l], ln2_b[l], slopes)
        pool0 = jnp.zeros((Bp, POOL_BUF, POOL_WIDTH), x_prompt.dtype)
        conv0 = jnp.zeros((Bp, CONV_BUF, CONV_WIDTH), x_prompt.dtype)
        y_p, ps, cs, kn, vn = decoder_layer(y_p, pool0, conv0, None, None, 0, *lw)
        p_pool.append(ps)
        p_conv.append(cs)
        p_k.append(kn[:, -keep:])
        p_v.append(vn[:, -keep:])
        y_s, ps2, cs2, kn2, vn2 = decoder_layer(y_s, state_pool[l], state_conv[l], cache_attn_k[l],
                                                cache_attn_v[l], PAST_LEN, *lw)
        s_pool.append(ps2)
        s_conv.append(cs2)
        s_k.append(kn2)
        s_v.append(vn2)
    return (y_p, y_s, jnp.stack(p_pool), jnp.stack(p_conv), jnp.stack(p_k), jnp.stack(p_v),
            jnp.stack(s_pool), jnp.stack(s_conv), jnp.stack(s_k), jnp.stack(s_v))
```

```python
import functools
import math

import numpy as np
import jax
import jax.numpy as jnp
from jax import lax
from jax.experimental import pallas as pl
from jax.experimental.pallas import tpu as pltpu

F32 = jnp.float32
BF16 = jnp.bfloat16

D_MODEL = 1024
BATCH = 4
SEQ = 4096
DEPTH = 4
DEC_BATCH = 32
PAST_LEN = 8192
POOL_WIDTH = 256
POOL_WINDOWS = (2, 4, 8, 16)
POOL_BUF = 15
ATTN_HEADS = 6
HEAD_DIM = 64
ATTN_WIDTH = ATTN_HEADS * HEAD_DIM
DILATED = ((128, 1), (512, 4), (2048, 16))
ATTN_BLOCK = 128
CACHE_LEN = 2048
CONV_WIDTH = 384
CONV_KERNEL = 31
CONV_BUF = 30
N_EXPERTS = 32
TOP_K = 4
D_EXPERT = 1024
SWIGLU_ALPHA = 1.702
SWIGLU_LIMIT = 7.0
DEEPNORM_ALPHA = (2 * DEPTH) ** 0.25
LN_EPS = 1e-5
NEG_BIG = -1e30
LANES = 128

M_PROMPT = BATCH * SEQ
TOK_TILE = 256
M_TOT = M_PROMPT + TOK_TILE
N_TOK_TILES = M_TOT // TOK_TILE
SEQ_TILE = 512
ATTN_Q_TILE = 512
EXPERT_TILE = 512
N_ASSIGN = M_TOT * TOP_K
N_EXPERT_TILES = -(-N_ASSIGN // EXPERT_TILE) + N_EXPERTS
N_SORTED = N_EXPERT_TILES * EXPERT_TILE
VMEM_LIMIT = 56 * 1024 * 1024


def _alibi_slopes(n):
    def pow2(k):
        start = 2.0 ** (-8.0 / k)
        return [start ** (i + 1) for i in range(k)]
    if math.log2(n).is_integer():
        return pow2(n)
    c = 2 ** math.floor(math.log2(n))
    return pow2(c) + pow2(2 * c)[0::2][: n - c]


def _bdot(a, b):
    return jnp.dot(a.astype(BF16), b.astype(BF16), preferred_element_type=F32)


def _layer_norm(x, g, b):
    mu = jnp.mean(x, axis=-1, keepdims=True)
    xc = x - mu
    var = jnp.mean(xc * xc, axis=-1, keepdims=True)
    return xc * lax.rsqrt(var + LN_EPS) * g + b


def _params(*sem):
    return pltpu.CompilerParams(dimension_semantics=sem, vmem_limit_bytes=VMEM_LIMIT)


def _in_proj_kernel(x_ref, w_ref, pool_ref, qkv_ref, conv_ref, wb_ref):
    @pl.when(pl.program_id(0) == 0)
    def _():
        wb_ref[...] = w_ref[...].astype(BF16)

    res = jnp.dot(x_ref[...].astype(BF16), wb_ref[...], preferred_element_type=F32)
    pool_ref[...] = res[:, :POOL_WIDTH]
    qkv_ref[...] = res[:, POOL_WIDTH:POOL_WIDTH + 3 * ATTN_WIDTH]
    conv_ref[...] = res[:, POOL_WIDTH + 3 * ATTN_WIDTH:]


def _in_proj(x, w_in):
    n_in = w_in.shape[1]
    return pl.pallas_call(
        _in_proj_kernel,
        grid=(N_TOK_TILES,),
        in_specs=[pl.BlockSpec((TOK_TILE, D_MODEL), lambda i: (i, 0)),
                  pl.BlockSpec((D_MODEL, n_in), lambda i: (0, 0), pipeline_mode=pl.Buffered(1))],
        out_specs=[pl.BlockSpec((TOK_TILE, POOL_WIDTH), lambda i: (i, 0)),
                   pl.BlockSpec((TOK_TILE, 3 * ATTN_WIDTH), lambda i: (i, 0)),
                   pl.BlockSpec((TOK_TILE, 2 * CONV_WIDTH), lambda i: (i, 0))],
        out_shape=[jax.ShapeDtypeStruct((M_TOT, POOL_WIDTH), F32),
                   jax.ShapeDtypeStruct((M_TOT, 3 * ATTN_WIDTH), F32),
                   jax.ShapeDtypeStruct((M_TOT, 2 * CONV_WIDTH), F32)],
        scratch_shapes=[pltpu.VMEM((D_MODEL, n_in), BF16)],
        compiler_params=_params("arbitrary"),
        name="in_proj",
    )(x, w_in)


def _pool_select(lane, vals):
    out = vals[-1]
    for g in range(len(vals) - 2, -1, -1):
        out = jnp.where(lane < (g + 1) * 64, vals[g], out)
    return out


def _pool_prompt_kernel(u_ref, w_ref, scale_ref, o_ref, halo_ref):
    t = pl.program_id(1)

    @pl.when(t == 0)
    def _():
        halo_ref[...] = jnp.zeros_like(halo_ref)

    u = u_ref[...]
    n = SEQ_TILE + 16
    ext = jnp.concatenate([halo_ref[...], u], axis=0)
    p2 = ext + pltpu.roll(ext, 1, 0)
    p4 = p2 + pltpu.roll(p2, 2, 0)
    p8 = p4 + pltpu.roll(p4, 4, 0)
    p16 = p8 + pltpu.roll(p8, 8, 0)
    lane = lax.broadcasted_iota(jnp.int32, (SEQ_TILE, POOL_WIDTH), 1)
    row = lax.broadcasted_iota(jnp.int32, (SEQ_TILE, POOL_WIDTH), 0)
    pos1 = (row + t * SEQ_TILE + 1).astype(F32)
    wsel = _pool_select(lane, [jnp.full((SEQ_TILE, POOL_WIDTH), float(w), F32) for w in POOL_WINDOWS])
    cnt = jnp.minimum(wsel, pos1)
    sums = _pool_select(lane, [p[16:n] for p in (p2, p4, p8, p16)])
    d = sums / cnt - u
    o_ref[...] = _bdot(d, w_ref[...]) * scale_ref[...]
    halo_ref[...] = u[SEQ_TILE - 16:]


def _pool_prompt(u_pool, w_bd, scale):
    nt = SEQ // SEQ_TILE
    return pl.pallas_call(
        _pool_prompt_kernel,
        grid=(BATCH, nt),
        in_specs=[pl.BlockSpec((SEQ_TILE, POOL_WIDTH), lambda b, t: (b * nt + t, 0)),
                  pl.BlockSpec((POOL_WIDTH, POOL_WIDTH), lambda b, t: (0, 0)),
                  pl.BlockSpec((1, POOL_WIDTH), lambda b, t: (0, 0))],
        out_specs=pl.BlockSpec((SEQ_TILE, POOL_WIDTH), lambda b, t: (b * nt + t, 0)),
        out_shape=jax.ShapeDtypeStruct((M_PROMPT, POOL_WIDTH), F32),
        scratch_shapes=[pltpu.VMEM((16, POOL_WIDTH), F32)],
        compiler_params=_params("arbitrary", "arbitrary"),
        name="pool_prompt",
    )(u_pool, w_bd, scale)


CONV_HALO = 32
CONV_CHUNK = 64


def _conv_prompt_kernel(c_ref, w_ref, b_ref, g_ref, bb_ref, y_ref, u_ref, ext_ref, sh_ref):
    t = pl.program_id(1)

    @pl.when(t == 0)
    def _():
        ext_ref[0:CONV_HALO, :] = jnp.zeros((CONV_HALO, CONV_WIDTH), F32)

    c = c_ref[...]
    u = c[:, :CONV_WIDTH] * jax.nn.sigmoid(c[:, CONV_WIDTH:])
    u_ref[...] = u
    ext_ref[CONV_HALO:CONV_HALO + SEQ_TILE, :] = u
    span = SEQ_TILE + CONV_HALO - 8
    for ph in range(1, 8):
        sh_ref[ph, 0:span, :] = ext_ref[pl.ds(ph, span), :]

    def chunk(ci, carry):
        base = pl.multiple_of(ci * CONV_CHUNK, CONV_CHUNK)
        acc = jnp.zeros((CONV_CHUNK, CONV_WIDTH), F32)
        for j in range(CONV_KERNEL):
            off = j + 2
            a, ph = off // 8, off % 8
            if ph == 0:
                tap = ext_ref[pl.ds(base + 8 * a, CONV_CHUNK), :]
            else:
                tap = sh_ref[ph, pl.ds(base + 8 * a, CONV_CHUNK), :]
            acc = acc + tap * w_ref[j:j + 1, :]
        y = _layer_norm(acc + b_ref[...], g_ref[...], bb_ref[...])
        y_ref[pl.ds(base, CONV_CHUNK), :] = y * jax.nn.sigmoid(y)
        return carry

    lax.fori_loop(0, SEQ_TILE // CONV_CHUNK, chunk, 0)
    ext_ref[0:CONV_HALO, :] = u[SEQ_TILE - CONV_HALO:]


def _conv_prompt(c_in, conv_w, conv_b, ln_g, ln_b):
    nt = SEQ // SEQ_TILE
    vec = pl.BlockSpec((1, CONV_WIDTH), lambda b, t: (0, 0))
    return pl.pallas_call(
        _conv_prompt_kernel,
        grid=(BATCH, nt),
        in_specs=[pl.BlockSpec((SEQ_TILE, 2 * CONV_WIDTH), lambda b, t: (b * nt + t, 0)),
                  pl.BlockSpec((CONV_KERNEL, CONV_WIDTH), lambda b, t: (0, 0)),
                  vec, vec, vec],
        out_specs=[pl.BlockSpec((SEQ_TILE, CONV_WIDTH), lambda b, t: (b * nt + t, 0)),
                   pl.BlockSpec((SEQ_TILE, CONV_WIDTH), lambda b, t: (b * nt + t, 0))],
        out_shape=[jax.ShapeDtypeStruct((M_PROMPT, CONV_WIDTH), F32),
                   jax.ShapeDtypeStruct((M_PROMPT, CONV_WIDTH), F32)],
        scratch_shapes=[pltpu.VMEM((SEQ_TILE + CONV_HALO, CONV_WIDTH), F32),
                        pltpu.VMEM((8, SEQ_TILE + CONV_HALO, CONV_WIDTH), F32)],
        compiler_params=_params("arbitrary", "arbitrary"),
        name="conv_prompt",
    )(c_in, conv_w, conv_b, ln_g, ln_b)


def _attn_bias_tables():
    slopes = _alibi_slopes(ATTN_HEADS)
    qi = np.arange(ATTN_BLOCK)[:, None] + ATTN_BLOCK
    ki = np.arange(2 * ATTN_BLOCK)[None, :]
    delta = qi - ki
    tabs = np.zeros((len(DILATED), ATTN_HEADS // 2, 4, ATTN_BLOCK, 2 * ATTN_BLOCK), np.float32)
    for bi, (w, d) in enumerate(DILATED):
        n_back = w // d
        band = (delta >= 0) & (delta <= n_back)
        for h in range(ATTN_HEADS):
            bias = -slopes[h] * (delta * d).astype(np.float32)
            for first in range(2):
                valid = band & (ki >= ATTN_BLOCK) if first else band
                tabs[bi, h // 2, 2 * (h % 2) + first] = np.where(valid, bias, NEG_BIG)
    return tabs


def _attn_prompt_kernel(q_ref, kp_ref, kc_ref, vp_ref, vc_ref, bias_ref, o_ref, lse_ref):
    br = pl.program_id(0)
    t = pl.program_id(2)
    blocks_per_seq = jnp.where(br == 0, SEQ // ATTN_BLOCK,
                               jnp.where(br == 1, SEQ // 4 // ATTN_BLOCK, SEQ // 16 // ATTN_BLOCK))
    kk = jnp.concatenate([kp_ref[...], kc_ref[...]], axis=0)
    vv = jnp.concatenate([vp_ref[...], vc_ref[...]], axis=0)
    lane_q = lax.broadcasted_iota(jnp.int32, (ATTN_BLOCK, 2 * HEAD_DIM), 1)
    for j in range(ATTN_Q_TILE // ATTN_BLOCK):
        blk = t * (ATTN_Q_TILE // ATTN_BLOCK) + j
        first = (blk % blocks_per_seq == 0).astype(jnp.int32)
        q = q_ref[j * ATTN_BLOCK:(j + 1) * ATTN_BLOCK, :]
        keys = kk[j * ATTN_BLOCK:(j + 2) * ATTN_BLOCK]
        vals = vv[j * ATTN_BLOCK:(j + 2) * ATTN_BLOCK]
        outs, lses = [], []
        for h in range(2):
            in_head = (lane_q >= h * HEAD_DIM) & (lane_q < (h + 1) * HEAD_DIM)
            qh = jnp.where(in_head, q, jnp.zeros_like(q))
            s = lax.dot_general(qh, keys, (((1,), (1,)), ((), ())), preferred_element_type=F32)
            s = s * (HEAD_DIM ** -0.5) + bias_ref[2 * h + first]
            m = jnp.max(s, axis=-1, keepdims=True)
            p = jnp.exp(s - m)
            l = jnp.sum(p, axis=-1, keepdims=True)
            acc = jnp.dot(p.astype(BF16), vals, preferred_element_type=F32)
            outs.append(acc / l)
            lses.append(m + jnp.log(l))
        o_ref[j * ATTN_BLOCK:(j + 1) * ATTN_BLOCK, :] = jnp.where(lane_q < HEAD_DIM, outs[0], outs[1])
        lse_ref[j * ATTN_BLOCK:(j + 1) * ATTN_BLOCK, :] = jnp.where(lane_q < HEAD_DIM, lses[0], lses[1])


def _attn_prompt(qkv_br, bias_tabs):
    nb = len(DILATED)
    n_hp = ATTN_HEADS // 2
    per_tile = ATTN_Q_TILE // ATTN_BLOCK
    nt = M_PROMPT // ATTN_Q_TILE
    width = 2 * HEAD_DIM
    cur = lambda off: pl.BlockSpec((None, ATTN_Q_TILE, width), lambda br, hp, t: (br, t, off * n_hp + hp))
    prev = lambda off: pl.BlockSpec((None, ATTN_BLOCK, width),
                                    lambda br, hp, t: (br, jnp.maximum(t * per_tile - 1, 0), off * n_hp + hp))
    out = pl.BlockSpec((None, ATTN_Q_TILE, width), lambda br, hp, t: (br, t, hp))
    return pl.pallas_call(
        _attn_prompt_kernel,
        grid=(nb, n_hp, nt),
        in_specs=[cur(0), prev(1), cur(1), prev(2), cur(2),
                  pl.BlockSpec((None, None, 4, ATTN_BLOCK, 2 * ATTN_BLOCK), lambda br, hp, t: (br, hp, 0, 0, 0))],
        out_specs=[out, out],
        out_shape=[jax.ShapeDtypeStruct((nb, M_PROMPT, ATTN_WIDTH), F32),
                   jax.ShapeDtypeStruct((nb, M_PROMPT, ATTN_WIDTH), F32)],
        compiler_params=_params("arbitrary", "arbitrary", "arbitrary"),
        name="attn_prompt",
    )(qkv_br, qkv_br, qkv_br, qkv_br, qkv_br, bias_tabs)


def _attn_merge_kernel(o_ref, lse_ref, y_ref):
    lse = lse_ref[...]
    w = jnp.exp(lse - jnp.max(lse, axis=0, keepdims=True))
    y_ref[...] = jnp.sum(w * o_ref[...], axis=0) / jnp.sum(w, axis=0)


def _attn_merge(o_br, lse_br):
    nb = len(DILATED)
    spec = pl.BlockSpec((nb, SEQ_TILE, ATTN_WIDTH), lambda i: (0, i, 0))
    return pl.pallas_call(
        _attn_merge_kernel,
        grid=(M_PROMPT // SEQ_TILE,),
        in_specs=[spec, spec],
        out_specs=pl.BlockSpec((SEQ_TILE, ATTN_WIDTH), lambda i: (i, 0)),
        out_shape=jax.ShapeDtypeStruct((M_PROMPT, ATTN_WIDTH), F32),
        compiler_params=_params("arbitrary"),
        name="attn_merge",
    )(o_br, lse_br)


def _mix_decode_kernel(u_ref, c_ref, sp_ref, sc_ref, pw_ref, ps_ref, cw_ref, cb_ref, cg_ref, cbb_ref,
                       ya_ref, yc_ref, glu_ref):
    nb = DEC_BATCH
    u = u_ref[0:nb, :]
    lane = lax.broadcasted_iota(jnp.int32, (nb, POOL_WIDTH), 1)
    run = u
    sums = []
    for back in range(1, max(POOL_WINDOWS)):
        run = run + sp_ref[POOL_BUF - back]
        if back + 1 in POOL_WINDOWS:
            sums.append(run)
    wsel = _pool_select(lane, [jnp.full((nb, POOL_WIDTH), float(w), F32) for w in POOL_WINDOWS])
    d = _pool_select(lane, sums) / wsel - u
    ya_ref[...] = jnp.zeros_like(ya_ref)
    ya_ref[0:nb, :] = _bdot(d, pw_ref[...]) * ps_ref[...]

    c = c_ref[0:nb, :]
    glu = c[:, :CONV_WIDTH] * jax.nn.sigmoid(c[:, CONV_WIDTH:])
    glu_ref[...] = glu
    acc = glu * cw_ref[CONV_KERNEL - 1:CONV_KERNEL, :]
    for j in range(CONV_BUF):
        acc = acc + sc_ref[j] * cw_ref[j:j + 1, :]
    y = _layer_norm(acc + cb_ref[...], cg_ref[...], cbb_ref[...])
    yc_ref[...] = jnp.zeros_like(yc_ref)
    yc_ref[0:nb, :] = y * jax.nn.sigmoid(y)


def _mix_decode(u_pool, c_in, sp_t, sc_t, w_bd, scale, conv_w, conv_b, ln_g, ln_b):
    last = N_TOK_TILES - 1
    full = lambda shape: pl.BlockSpec(shape, lambda i: (0,) * len(shape))
    return pl.pallas_call(
        _mix_decode_kernel,
        grid=(1,),
        in_specs=[pl.BlockSpec((TOK_TILE, POOL_WIDTH), lambda i: (last, 0)),
                  pl.BlockSpec((TOK_TILE, 2 * CONV_WIDTH), lambda i: (last, 0)),
                  full((POOL_BUF, DEC_BATCH, POOL_WIDTH)), full((CONV_BUF, DEC_BATCH, CONV_WIDTH)),
                  full((POOL_WIDTH, POOL_WIDTH)), full((1, POOL_WIDTH)),
                  full((CONV_KERNEL, CONV_WIDTH)), full((1, CONV_WIDTH)), full((1, CONV_WIDTH)),
                  full((1, CONV_WIDTH))],
        out_specs=[full((TOK_TILE, POOL_WIDTH)), full((TOK_TILE, CONV_WIDTH)), full((DEC_BATCH, CONV_WIDTH))],
        out_shape=[jax.ShapeDtypeStruct((TOK_TILE, POOL_WIDTH), F32),
                   jax.ShapeDtypeStruct((TOK_TILE, CONV_WIDTH), F32),
                   jax.ShapeDtypeStruct((DEC_BATCH, CONV_WIDTH), F32)],
        compiler_params=_params("arbitrary"),
        name="mix_decode",
    )(u_pool, c_in, sp_t, sc_t, w_bd, scale, conv_w, conv_b, ln_g, ln_b)


def _attn_decode_bias():
    slopes = _alibi_slopes(ATTN_HEADS)
    dist = np.concatenate([CACHE_LEN - (CACHE_LEN - w + d * np.arange(ATTN_BLOCK)) for w, d in DILATED])
    tab = np.zeros((8, dist.size), np.float32)
    for h in range(ATTN_HEADS):
        tab[h] = -slopes[h] * dist.astype(np.float32)
    return tab


CACHE_CHUNKS = ATTN_WIDTH // LANES


def _strided_cache_rows(ref):
    return jnp.concatenate(
        [jnp.concatenate([ref[pl.ds(CACHE_CHUNKS * (CACHE_LEN - w) + c, ATTN_BLOCK, stride=CACHE_CHUNKS * d), :]
                          for c in range(CACHE_CHUNKS)], axis=1) for w, d in DILATED], axis=0)


def _attn_decode_kernel(qkv_ref, k_ref, v_ref, bias_ref, y_ref):
    qkv = qkv_ref[...]
    q = qkv[:, :ATTN_WIDTH]
    k_new = qkv[:, ATTN_WIDTH:2 * ATTN_WIDTH]
    v_new = qkv[:, 2 * ATTN_WIDTH:]
    kc = _strided_cache_rows(k_ref)
    vc = _strided_cache_rows(v_ref)
    row = lax.broadcasted_iota(jnp.int32, (8, ATTN_WIDTH), 0)
    lane = lax.broadcasted_iota(jnp.int32, (8, ATTN_WIDTH), 1)
    own = (lane >= row * HEAD_DIM) & (lane < (row + 1) * HEAD_DIM)
    qb = q.astype(BF16).astype(F32)
    q_rows = jnp.where(own, jnp.broadcast_to(qb, (8, ATTN_WIDTH)), 0.0)
    scale = HEAD_DIM ** -0.5
    s = lax.dot_general(q_rows.astype(BF16), kc.astype(BF16), (((1,), (1,)), ((), ())),
                        preferred_element_type=F32) * scale + bias_ref[...]
    s_self = jnp.sum(q_rows * k_new.astype(BF16).astype(F32), axis=-1, keepdims=True) * scale
    m = jnp.maximum(jnp.max(s, axis=-1, keepdims=True), s_self)
    p = jnp.exp(s - m)
    p_self = len(DILATED) * jnp.exp(s_self - m)
    l = jnp.sum(p, axis=-1, keepdims=True) + p_self
    acc = jnp.dot(p.astype(BF16), vc.astype(BF16), preferred_element_type=F32)
    acc = acc + p_self.astype(BF16).astype(F32) * v_new.astype(BF16).astype(F32)
    y_ref[...] = jnp.sum(jnp.where(own, acc / l, 0.0), axis=0, keepdims=True)


def _attn_decode(qkv_d, cache_k, cache_v, bias):
    return pl.pallas_call(
        _attn_decode_kernel,
        grid=(DEC_BATCH,),
        in_specs=[pl.BlockSpec((None, 1, 3 * ATTN_WIDTH), lambda b: (b, 0, 0)),
                  pl.BlockSpec((None, CACHE_LEN * CACHE_CHUNKS, LANES), lambda b: (b, 0, 0)),
                  pl.BlockSpec((None, CACHE_LEN * CACHE_CHUNKS, LANES), lambda b: (b, 0, 0)),
                  pl.BlockSpec((8, 3 * ATTN_BLOCK), lambda b: (0, 0))],
        out_specs=pl.BlockSpec((None, 1, ATTN_WIDTH), lambda b: (b, 0, 0)),
        out_shape=jax.ShapeDtypeStruct((DEC_BATCH, 1, ATTN_WIDTH), F32),
        compiler_params=_params("arbitrary"),
        name="attn_decode",
    )(qkv_d, cache_k, cache_v, bias)


def _post_mix_kernel(x_ref, mix_ref, wo_ref, g_ref, b_ref, rw_ref, rb_ref, tri_ref,
                     x1_ref, info_ref, gate_ref, cnt_ref, wob_ref, run_ref):
    @pl.when(pl.program_id(0) == 0)
    def _():
        wob_ref[...] = wo_ref[...].astype(BF16)
        run_ref[...] = jnp.zeros_like(run_ref)

    mix = jnp.dot(mix_ref[...].astype(BF16), wob_ref[...], preferred_element_type=F32)
    x1 = _layer_norm(DEEPNORM_ALPHA * x_ref[...] + mix, g_ref[...], b_ref[...])
    x1_ref[...] = x1
    logits = _bdot(x1, rw_ref[...]) + rb_ref[...]
    lane = lax.broadcasted_iota(jnp.int32, (TOK_TILE, LANES), 1).astype(F32)
    work = logits
    chosen = jnp.zeros((TOK_TILE, LANES), F32)
    tops, idxs, hots = [], [], []
    for _ in range(TOP_K):
        mk = jnp.max(work, axis=-1, keepdims=True)
        idx = jnp.min(jnp.where(work == mk, lane, float(LANES)), axis=-1, keepdims=True)
        hot = lane == idx
        tops.append(mk)
        idxs.append(idx)
        hots.append(hot)
        chosen = chosen + jnp.where(hot, 1.0, 0.0)
        work = jnp.where(hot, -jnp.inf, work)
    exps = [jnp.exp(v - tops[0]) for v in tops]
    denom = exps[0] + exps[1] + exps[2] + exps[3]
    ahead = jnp.dot(tri_ref[...], chosen.astype(BF16), preferred_element_type=F32) + run_ref[0:1, :]
    ranks = [jnp.sum(jnp.where(hot, ahead, 0.0), axis=-1, keepdims=True) for hot in hots]
    info = jnp.zeros((TOK_TILE, LANES), F32)
    gates = jnp.zeros((TOK_TILE, LANES), F32)
    for k in range(TOP_K):
        info = jnp.where(lane == float(k), idxs[k], info)
        info = jnp.where(lane == float(TOP_K + k), ranks[k], info)
        gates = jnp.where(lane == float(k), exps[k] / denom, gates)
    info_ref[...] = info.astype(jnp.int32)
    gate_ref[...] = gates
    run_ref[...] = run_ref[...] + jnp.sum(chosen, axis=0, keepdims=True)
    cnt_ref[...] = run_ref[...]


def _post_mix(x, mix, w_o, ln_g, ln_b, router_w, router_b, tri):
    row = pl.BlockSpec((TOK_TILE, D_MODEL), lambda i: (i, 0))
    vec = pl.BlockSpec((1, D_MODEL), lambda i: (0, 0))
    lanes = pl.BlockSpec((TOK_TILE, LANES), lambda i: (i, 0))
    return pl.pallas_call(
        _post_mix_kernel,
        grid=(N_TOK_TILES,),
        in_specs=[row, row,
                  pl.BlockSpec((D_MODEL, D_MODEL), lambda i: (0, 0), pipeline_mode=pl.Buffered(1)),
                  vec, vec,
                  pl.BlockSpec((D_MODEL, LANES), lambda i: (0, 0)),
                  pl.BlockSpec((1, LANES), lambda i: (0, 0)),
                  pl.BlockSpec((TOK_TILE, TOK_TILE), lambda i: (0, 0))],
        out_specs=[row, lanes, lanes, pl.BlockSpec((8, LANES), lambda i: (0, 0))],
        out_shape=[jax.ShapeDtypeStruct((M_TOT, D_MODEL), F32),
                   jax.ShapeDtypeStruct((M_TOT, LANES), jnp.int32),
                   jax.ShapeDtypeStruct((M_TOT, LANES), F32),
                   jax.ShapeDtypeStruct((8, LANES), F32)],
        scratch_shapes=[pltpu.VMEM((D_MODEL, D_MODEL), BF16), pltpu.VMEM((8, LANES), F32)],
        compiler_params=_params("arbitrary"),
        name="post_mix",
    )(x, mix, w_o, ln_g, ln_b, router_w, router_b, tri)


def _row_copy(src, src_row, dst, dst_row, sem):
    return pltpu.make_async_copy(src.at[pl.ds(src_row, 1), :], dst.at[pl.ds(dst_row, 1), :], sem)


def _dispatch_kernel(dest_ref, fill_ref, nu_ref, x_hbm, xs_hbm, zero_ref, sem):
    i = pl.program_id(0)
    rows_per_step = TOK_TILE * TOP_K

    def wait_rows(n):
        pltpu.make_async_copy(x_hbm.at[pl.ds(0, n), :], xs_hbm.at[pl.ds(0, n), :], sem).wait()

    @pl.when(i == 0)
    def _():
        zero_ref[...] = jnp.zeros_like(zero_ref)

        def fill(e, carry):
            @pl.when(fill_ref[e] >= 0)
            def _():
                pltpu.make_async_copy(zero_ref, xs_hbm.at[pl.ds(pl.multiple_of(fill_ref[e], EXPERT_TILE), EXPERT_TILE), :], sem).start()
            return carry

        def drain(e, carry):
            @pl.when(fill_ref[e] >= 0)
            def _():
                pltpu.make_async_copy(zero_ref, xs_hbm.at[pl.ds(pl.multiple_of(fill_ref[e], EXPERT_TILE), EXPERT_TILE), :], sem).wait()
            return carry

        def tail_copy(tile):
            start = pl.multiple_of(tile * EXPERT_TILE, EXPERT_TILE)
            return pltpu.make_async_copy(zero_ref, xs_hbm.at[pl.ds(start, EXPERT_TILE), :], sem)

        def fill_tail(tile, carry):
            tail_copy(tile).start()
            return carry

        def drain_tail(tile, carry):
            tail_copy(tile).wait()
            return carry

        lax.fori_loop(0, N_EXPERTS, fill, 0)
        lax.fori_loop(nu_ref[0], N_EXPERT_TILES, fill_tail, 0)
        lax.fori_loop(0, N_EXPERTS, drain, 0)
        lax.fori_loop(nu_ref[0], N_EXPERT_TILES, drain_tail, 0)

    def issue(r, carry):
        tok = i * TOK_TILE + r
        for k in range(TOP_K):
            _row_copy(x_hbm, tok, xs_hbm, dest_ref[tok * TOP_K + k], sem).start()
        return carry

    lax.fori_loop(0, TOK_TILE, issue, 0, unroll=8)

    @pl.when(i > 0)
    def _():
        wait_rows(rows_per_step)

    @pl.when(i == N_TOK_TILES - 1)
    def _():
        wait_rows(rows_per_step)


def _dispatch(dest_flat, fill_start, n_used, x1):
    return pl.pallas_call(
        _dispatch_kernel,
        grid_spec=pltpu.PrefetchScalarGridSpec(
            num_scalar_prefetch=3,
            grid=(N_TOK_TILES,),
            in_specs=[pl.BlockSpec(memory_space=pl.ANY)],
            out_specs=pl.BlockSpec(memory_space=pl.ANY),
            scratch_shapes=[pltpu.VMEM((EXPERT_TILE, D_MODEL), F32), pltpu.SemaphoreType.DMA]),
        out_shape=jax.ShapeDtypeStruct((N_SORTED, D_MODEL), F32),
        compiler_params=_params("arbitrary"),
        name="moe_dispatch",
    )(dest_flat, fill_start, n_used, x1)


def _experts_kernel(te_ref, nu_ref, x_ref, w1_ref, b1_ref, w2_ref, b2_ref, o_ref, w1b_ref, w2b_ref):
    i = pl.program_id(0)
    e = te_ref[i]
    e_prev = te_ref[jnp.maximum(i - 1, 0)]

    @pl.when((i == 0) | (e != e_prev))
    def _():
        w1b_ref[...] = w1_ref[...].astype(BF16)
        w2b_ref[...] = w2_ref[...].astype(BF16)

    @pl.when(i < nu_ref[0])
    def _():
        h = jnp.dot(x_ref[...].astype(BF16), w1b_ref[...], preferred_element_type=F32) + b1_ref[...]
        gate = jnp.minimum(h[:, :D_EXPERT], SWIGLU_LIMIT)
        up = jnp.clip(h[:, D_EXPERT:], -SWIGLU_LIMIT, SWIGLU_LIMIT)
        act = gate * jax.nn.sigmoid(SWIGLU_ALPHA * gate) * (up + 1.0)
        o_ref[...] = jnp.dot(act.astype(BF16), w2b_ref[...], preferred_element_type=F32) + b2_ref[...]

    @pl.when(i >= nu_ref[0])
    def _():
        o_ref[...] = jnp.zeros_like(o_ref)


def _experts(tile_expert, n_used, xs, w1, b1, w2, b2):
    by_expert = lambda *shape: pl.BlockSpec((None,) + shape, lambda i, te, nu: (te[i], 0, 0))
    rows = pl.BlockSpec((EXPERT_TILE, D_MODEL), lambda i, te, nu: (jnp.minimum(i, nu[0] - 1), 0))
    return pl.pallas_call(
        _experts_kernel,
        grid_spec=pltpu.PrefetchScalarGridSpec(
            num_scalar_prefetch=2,
            grid=(N_EXPERT_TILES,),
            in_specs=[rows, by_expert(D_MODEL, 2 * D_EXPERT), by_expert(1, 2 * D_EXPERT),
                      by_expert(D_EXPERT, D_MODEL), by_expert(1, D_MODEL)],
            out_specs=pl.BlockSpec((EXPERT_TILE, D_MODEL), lambda i, te, nu: (i, 0)),
            scratch_shapes=[pltpu.VMEM((D_MODEL, 2 * D_EXPERT), BF16), pltpu.VMEM((D_EXPERT, D_MODEL), BF16)]),
        out_shape=jax.ShapeDtypeStruct((N_SORTED, D_MODEL), F32),
        compiler_params=_params("arbitrary"),
        name="moe_experts",
    )(tile_expert, n_used, xs, w1, b1, w2, b2)


def _combine_kernel(dest_ref, x1_ref, gate_ref, g_ref, b_ref, ys_hbm, o_ref, rows_ref, sem):
    i = pl.program_id(0)
    rows_per_step = TOK_TILE * TOP_K

    def gather(tile, slot):
        def issue(r, carry):
            tok = tile * TOK_TILE + r
            for k in range(TOP_K):
                pltpu.make_async_copy(ys_hbm.at[pl.ds(dest_ref[tok * TOP_K + k], 1), :],
                                      rows_ref.at[slot, k, pl.ds(r, 1), :], sem.at[slot]).start()
            return carry
        lax.fori_loop(0, TOK_TILE, issue, 0, unroll=8)

    @pl.when(i == 0)
    def _():
        gather(0, 0)

    @pl.when(i + 1 < N_TOK_TILES)
    def _():
        gather(i + 1, (i + 1) % 2)

    slot = i % 2
    for k in range(TOP_K):
        pltpu.make_async_copy(ys_hbm.at[pl.ds(0, TOK_TILE), :], rows_ref.at[slot, k], sem.at[slot]).wait()
    gates = gate_ref[...]
    moe = gates[:, 0:1] * rows_ref[slot, 0]
    for k in range(1, TOP_K):
        moe = moe + gates[:, k:k + 1] * rows_ref[slot, k]
    o_ref[...] = _layer_norm(DEEPNORM_ALPHA * x1_ref[...] + moe, g_ref[...], b_ref[...])


def _combine(dest_flat, x1, gates, ln_g, ln_b, ys):
    row = pl.BlockSpec((TOK_TILE, D_MODEL), lambda i, d: (i, 0))
    vec = pl.BlockSpec((1, D_MODEL), lambda i, d: (0, 0))
    return pl.pallas_call(
        _combine_kernel,
        grid_spec=pltpu.PrefetchScalarGridSpec(
            num_scalar_prefetch=1,
            grid=(N_TOK_TILES,),
            in_specs=[row, pl.BlockSpec((TOK_TILE, LANES), lambda i, d: (i, 0)), vec, vec,
                      pl.BlockSpec(memory_space=pl.ANY)],
            out_specs=row,
            scratch_shapes=[pltpu.VMEM((2, TOP_K, TOK_TILE, D_MODEL), F32), pltpu.SemaphoreType.DMA((2,))]),
        out_shape=jax.ShapeDtypeStruct((M_TOT, D_MODEL), F32),
        compiler_params=_params("arbitrary"),
        name="moe_combine",
    )(dest_flat, x1, gates, ln_g, ln_b, ys)


def _routing_tables(info, counts):
    top_i = info[:, :TOP_K]
    rank = info[:, TOP_K:2 * TOP_K]
    cnt = counts[0, :N_EXPERTS].astype(jnp.int32)
    padded = (cnt + EXPERT_TILE - 1) // EXPERT_TILE * EXPERT_TILE
    pad_end = jnp.cumsum(padded)
    pad_start = pad_end - padded
    hot = top_i[:, :, None] == jnp.arange(N_EXPERTS, dtype=jnp.int32)
    dest = rank + jnp.sum(jnp.where(hot, pad_start, 0), axis=-1)
    fill_start = jnp.where(cnt % EXPERT_TILE != 0, pad_end - EXPERT_TILE, -1).astype(jnp.int32)
    tile_start = jnp.arange(N_EXPERT_TILES, dtype=jnp.int32) * EXPERT_TILE
    n_used = (pad_end[-1] // EXPERT_TILE).astype(jnp.int32)
    tile_expert = jnp.sum(tile_start[:, None] >= pad_end[None, :], axis=1).astype(jnp.int32)
    last_expert = tile_expert[jnp.maximum(n_used - 1, 0)]
    tile_expert = jnp.where(tile_start < pad_end[-1], tile_expert, last_expert)
    return dest.reshape(-1).astype(jnp.int32), fill_start, tile_expert, n_used.reshape(1)


def _deinterleave(x, d):
    c = x.shape[-1]
    return x.reshape(BATCH, SEQ // d, d, c).transpose(0, 2, 1, 3).reshape(M_PROMPT, c)


def _interleave(x, d):
    c = x.shape[-1]
    return x.reshape(BATCH, d, SEQ // d, c).transpose(0, 2, 1, 3).reshape(M_PROMPT, c)


def _block_diag(pool_w):
    g, n, _ = pool_w.shape
    out = jnp.zeros((g * n, g * n), pool_w.dtype)
    for i in range(g):
        out = out.at[i * n:(i + 1) * n, i * n:(i + 1) * n].set(pool_w[i])
    return out


def kernel(x_prompt, x_sample, state_pool, state_conv, cache_attn_k, cache_attn_v, w_in, pool_w, pool_scale, conv_w, conv_b, conv_ln_g, conv_ln_b, w_o, ln1_g, ln1_b, router_w, router_b, expert_w_in, expert_b_in, expert_w_out, expert_b_out, ln2_g, ln2_b):
    bias_prompt = jnp.asarray(_attn_bias_tables())
    bias_decode = jnp.asarray(_attn_decode_bias())
    tri = jnp.asarray(np.tril(np.ones((TOK_TILE, TOK_TILE), np.float32), -1), BF16)
    x = jnp.concatenate([x_prompt.reshape(M_PROMPT, D_MODEL), x_sample.reshape(DEC_BATCH, D_MODEL),
                         jnp.zeros((M_TOT - M_PROMPT - DEC_BATCH, D_MODEL), F32)], axis=0)
    keep = min(CACHE_LEN, SEQ)
    outs = {name: [] for name in ("p_pool", "p_conv", "p_k", "p_v", "s_pool", "s_conv", "s_k", "s_v")}
    for l in range(DEPTH):
        u_pool, qkv, c_in = _in_proj(x, w_in[l])
        w_bd = _block_diag(pool_w[l])
        scale = pool_scale[l].reshape(1, POOL_WIDTH)
        cb, cg, cbb = (v[l].reshape(1, CONV_WIDTH) for v in (conv_b, conv_ln_g, conv_ln_b))

        ya = _pool_prompt(u_pool, w_bd, scale)
        yc, glu = _conv_prompt(c_in, conv_w[l], cb, cg, cbb)
        qkv_p = qkv[:M_PROMPT]
        qkv_br = jnp.stack([_deinterleave(qkv_p, d) for _, d in DILATED]).astype(BF16)
        o_br, lse_br = _attn_prompt(qkv_br, bias_prompt)
        o_br = jnp.stack([_interleave(o_br[i], d) for i, (_, d) in enumerate(DILATED)])
        lse_br = jnp.stack([_interleave(lse_br[i], d) for i, (_, d) in enumerate(DILATED)])
        yb = _attn_merge(o_br, lse_br)

        sp_t = state_pool[l].transpose(1, 0, 2)
        sc_t = state_conv[l].transpose(1, 0, 2)
        ya_d, yc_d, glu_d = _mix_decode(u_pool, c_in, sp_t, sc_t, w_bd, scale, conv_w[l], cb, cg, cbb)
        qkv_d = qkv[M_PROMPT:M_PROMPT + DEC_BATCH]
        yb_d = _attn_decode(qkv_d.reshape(DEC_BATCH, 1, 3 * ATTN_WIDTH),
                            cache_attn_k[l].reshape(DEC_BATCH, CACHE_LEN * CACHE_CHUNKS, LANES),
                            cache_attn_v[l].reshape(DEC_BATCH, CACHE_LEN * CACHE_CHUNKS, LANES), bias_decode)
        yb_d = jnp.pad(yb_d.reshape(DEC_BATCH, ATTN_WIDTH), ((0, TOK_TILE - DEC_BATCH), (0, 0)))
        mix = jnp.concatenate([jnp.concatenate([ya, yb, yc], axis=1),
                               jnp.concatenate([ya_d, yb_d, yc_d], axis=1)], axis=0)

        rw = jnp.pad(router_w[l], ((0, 0), (0, LANES - N_EXPERTS)))
        rb = jnp.pad(router_b[l], (0, LANES - N_EXPERTS), constant_values=NEG_BIG).reshape(1, LANES)
        x1, info, gates, counts = _post_mix(x, mix, w_o[l], ln1_g[l].reshape(1, D_MODEL),
                                            ln1_b[l].reshape(1, D_MODEL), rw, rb, tri)

        dest, fill_start, tile_expert, n_used = _routing_tables(info, counts)
        xs = _dispatch(dest, fill_start, n_used, x1)
        ys = _experts(tile_expert, n_used, xs, expert_w_in[l], expert_b_in[l].reshape(N_EXPERTS, 1, 2 * D_EXPERT),
                      expert_w_out[l], expert_b_out[l].reshape(N_EXPERTS, 1, D_MODEL))
        x = _combine(dest, x1, gates, ln2_g[l].reshape(1, D_MODEL), ln2_b[l].reshape(1, D_MODEL), ys)

        k_p = qkv_p[:, ATTN_WIDTH:2 * ATTN_WIDTH].reshape(BATCH, SEQ, ATTN_HEADS, HEAD_DIM)
        v_p = qkv_p[:, 2 * ATTN_WIDTH:].reshape(BATCH, SEQ, ATTN_HEADS, HEAD_DIM)
        outs["p_pool"].append(u_pool[:M_PROMPT].reshape(BATCH, SEQ, POOL_WIDTH)[:, -POOL_BUF:])
        outs["p_conv"].append(glu.reshape(BATCH, SEQ, CONV_WIDTH)[:, -CONV_BUF:])
        outs["p_k"].append(k_p[:, -keep:])
        outs["p_v"].append(v_p[:, -keep:])
        u_d = u_pool[M_PROMPT:M_PROMPT + DEC_BATCH]
        outs["s_pool"].append(jnp.concatenate([state_pool[l][:, 1:], u_d[:, None]], axis=1))
        outs["s_conv"].append(jnp.concatenate([state_conv[l][:, 1:], glu_d[:, None]], axis=1))
        outs["s_k"].append(qkv_d[:, ATTN_WIDTH:2 * ATTN_WIDTH].reshape(DEC_BATCH, 1, ATTN_HEADS, HEAD_DIM))
        outs["s_v"].append(qkv_d[:, 2 * ATTN_WIDTH:].reshape(DEC_BATCH, 1, ATTN_HEADS, HEAD_DIM))

    y_p = x[:M_PROMPT].reshape(BATCH, SEQ, D_MODEL)
    y_s = x[M_PROMPT:M_PROMPT + DEC_BATCH].reshape(DEC_BATCH, 1, D_MODEL)
    return (y_p, y_s) + tuple(jnp.stack(outs[name]) for name in
                              ("p_pool", "p_conv", "p_k", "p_v", "s_pool", "s_conv", "s_k", "s_v"))
```

```python
import functools
import math

import numpy as np
import jax
import jax.numpy as jnp
from jax import lax
from jax.experimental import pallas as pl
from jax.experimental.pallas import tpu as pltpu

F32 = jnp.float32
BF16 = jnp.bfloat16

D_MODEL = 1024
BATCH = 4
SEQ = 4096
DEPTH = 4
DEC_BATCH = 32
PAST_LEN = 8192
POOL_WIDTH = 256
POOL_WINDOWS = (2, 4, 8, 16)
POOL_BUF = 15
ATTN_HEADS = 6
HEAD_DIM = 64
ATTN_WIDTH = ATTN_HEADS * HEAD_DIM
DILATED = ((128, 1), (512, 4), (2048, 16))
ATTN_BLOCK = 128
CACHE_LEN = 2048
CONV_WIDTH = 384
CONV_KERNEL = 31
CONV_BUF = 30
N_EXPERTS = 32
TOP_K = 4
D_EXPERT = 1024
SWIGLU_ALPHA = 1.702
SWIGLU_LIMIT = 7.0
DEEPNORM_ALPHA = (2 * DEPTH) ** 0.25
LN_EPS = 1e-5
NEG_BIG = -1e30
LANES = 128

M_PROMPT = BATCH * SEQ
TOK_TILE = 256
M_TOT = M_PROMPT + TOK_TILE
N_TOK_TILES = M_TOT // TOK_TILE
SEQ_TILE = 512
ATTN_Q_TILE = 512
EXPERT_TILE = 512
N_ASSIGN = M_TOT * TOP_K
N_EXPERT_TILES = -(-N_ASSIGN // EXPERT_TILE) + N_EXPERTS
N_SORTED = N_EXPERT_TILES * EXPERT_TILE
VMEM_LIMIT = 56 * 1024 * 1024


def _alibi_slopes(n):
    def pow2(k):
        start = 2.0 ** (-8.0 / k)
        return [start ** (i + 1) for i in range(k)]
    if math.log2(n).is_integer():
        return pow2(n)
    c = 2 ** math.floor(math.log2(n))
    return pow2(c) + pow2(2 * c)[0::2][: n - c]


def _bdot(a, b):
    return jnp.dot(a.astype(BF16), b.astype(BF16), preferred_element_type=F32)


def _layer_norm(x, g, b):
    mu = jnp.mean(x, axis=-1, keepdims=True)
    xc = x - mu
    var = jnp.mean(xc * xc, axis=-1, keepdims=True)
    return xc * lax.rsqrt(var + LN_EPS) * g + b


def _params(*sem):
    return pltpu.CompilerParams(dimension_semantics=sem, vmem_limit_bytes=VMEM_LIMIT)


N_PROMPT_TILES = M_PROMPT // TOK_TILE
TILES_PER_SEQ = SEQ // TOK_TILE
QKV_WIDTH = 3 * ATTN_WIDTH
QKV_CHUNKS = QKV_WIDTH // LANES


def _in_proj_kernel(x_ref, w_ref, pool_ref, qkv_ref, conv_ref, q1_ref, q4_ref, q16_ref, wb_ref, stage_ref):
    i = pl.program_id(0)

    @pl.when(i == 0)
    def _():
        wb_ref[...] = w_ref[...].astype(BF16)

    res = jnp.dot(x_ref[...].astype(BF16), wb_ref[...], preferred_element_type=F32)
    qkv = res[:, POOL_WIDTH:POOL_WIDTH + QKV_WIDTH]
    pool_ref[...] = res[:, :POOL_WIDTH]
    qkv_ref[...] = qkv
    conv_ref[...] = res[:, POOL_WIDTH + QKV_WIDTH:]

    @pl.when(i < N_PROMPT_TILES)
    def _():
        q1_ref[...] = qkv.astype(BF16)
        for c in range(QKV_CHUNKS):
            stage_ref[c] = qkv[:, c * LANES:(c + 1) * LANES]
        for ref, d in ((q4_ref, DILATED[1][1]), (q16_ref, DILATED[2][1])):
            n = TOK_TILE // d
            for r in range(d):
                rows = [stage_ref[c, pl.ds(r, n, stride=d), :] for c in range(QKV_CHUNKS)]
                ref[r] = jnp.concatenate(rows, axis=1).astype(BF16)


def _in_proj(x, w_in, layer):
    n_in = w_in.shape[2]
    d4, d16 = DILATED[1][1], DILATED[2][1]
    tile = lambda i: jnp.minimum(i, N_PROMPT_TILES - 1)
    by_residue = lambda d: pl.BlockSpec(
        (None, d, TOK_TILE // d, QKV_WIDTH), lambda i: (tile(i) // TILES_PER_SEQ, 0, tile(i) % TILES_PER_SEQ, 0))
    return pl.pallas_call(
        _in_proj_kernel,
        grid=(N_TOK_TILES,),
        in_specs=[pl.BlockSpec((TOK_TILE, D_MODEL), lambda i: (i, 0)),
                  pl.BlockSpec((None, D_MODEL, n_in), lambda i: (layer, 0, 0), pipeline_mode=pl.Buffered(1))],
        out_specs=[pl.BlockSpec((TOK_TILE, POOL_WIDTH), lambda i: (i, 0)),
                   pl.BlockSpec((TOK_TILE, QKV_WIDTH), lambda i: (i, 0)),
                   pl.BlockSpec((TOK_TILE, 2 * CONV_WIDTH), lambda i: (i, 0)),
                   pl.BlockSpec((TOK_TILE, QKV_WIDTH), lambda i: (tile(i), 0)),
                   by_residue(d4), by_residue(d16)],
        out_shape=[jax.ShapeDtypeStruct((M_TOT, POOL_WIDTH), F32),
                   jax.ShapeDtypeStruct((M_TOT, QKV_WIDTH), F32),
                   jax.ShapeDtypeStruct((M_TOT, 2 * CONV_WIDTH), F32),
                   jax.ShapeDtypeStruct((M_PROMPT, QKV_WIDTH), BF16),
                   jax.ShapeDtypeStruct((BATCH, d4, SEQ // d4, QKV_WIDTH), BF16),
                   jax.ShapeDtypeStruct((BATCH, d16, SEQ // d16, QKV_WIDTH), BF16)],
        scratch_shapes=[pltpu.VMEM((D_MODEL, n_in), BF16), pltpu.VMEM((QKV_CHUNKS, TOK_TILE, LANES), F32)],
        compiler_params=_params("arbitrary"),
        name="in_proj",
    )(x, w_in)


def _pool_select(lane, vals):
    out = vals[-1]
    for g in range(len(vals) - 2, -1, -1):
        out = jnp.where(lane < (g + 1) * 64, vals[g], out)
    return out


def _pool_prompt_kernel(u_ref, w_ref, scale_ref, o_ref, halo_ref):
    t = pl.program_id(1)

    @pl.when(t == 0)
    def _():
        halo_ref[...] = jnp.zeros_like(halo_ref)

    u = u_ref[...]
    n = SEQ_TILE + 16
    ext = jnp.concatenate([halo_ref[...], u], axis=0)
    p2 = ext + pltpu.roll(ext, 1, 0)
    p4 = p2 + pltpu.roll(p2, 2, 0)
    p8 = p4 + pltpu.roll(p4, 4, 0)
    p16 = p8 + pltpu.roll(p8, 8, 0)
    lane = lax.broadcasted_iota(jnp.int32, (SEQ_TILE, POOL_WIDTH), 1)
    row = lax.broadcasted_iota(jnp.int32, (SEQ_TILE, POOL_WIDTH), 0)
    pos1 = (row + t * SEQ_TILE + 1).astype(F32)
    wsel = _pool_select(lane, [jnp.full((SEQ_TILE, POOL_WIDTH), float(w), F32) for w in POOL_WINDOWS])
    cnt = jnp.minimum(wsel, pos1)
    sums = _pool_select(lane, [p[16:n] for p in (p2, p4, p8, p16)])
    d = sums / cnt - u
    o_ref[...] = _bdot(d, w_ref[...]) * scale_ref[...]
    halo_ref[...] = u[SEQ_TILE - 16:]


def _pool_prompt(u_pool, w_bd, scale):
    nt = SEQ // SEQ_TILE
    return pl.pallas_call(
        _pool_prompt_kernel,
        grid=(BATCH, nt),
        in_specs=[pl.BlockSpec((SEQ_TILE, POOL_WIDTH), lambda b, t: (b * nt + t, 0)),
                  pl.BlockSpec((POOL_WIDTH, POOL_WIDTH), lambda b, t: (0, 0)),
                  pl.BlockSpec((1, POOL_WIDTH), lambda b, t: (0, 0))],
        out_specs=pl.BlockSpec((SEQ_TILE, POOL_WIDTH), lambda b, t: (b * nt + t, 0)),
        out_shape=jax.ShapeDtypeStruct((M_PROMPT, POOL_WIDTH), F32),
        scratch_shapes=[pltpu.VMEM((16, POOL_WIDTH), F32)],
        compiler_params=_params("arbitrary", "arbitrary"),
        name="pool_prompt",
    )(u_pool, w_bd, scale)


CONV_HALO = 32
CONV_CHUNK = 64


def _conv_prompt_kernel(c_ref, w_ref, b_ref, g_ref, bb_ref, y_ref, u_ref, ext_ref, sh_ref):
    t = pl.program_id(1)

    @pl.when(t == 0)
    def _():
        ext_ref[0:CONV_HALO, :] = jnp.zeros((CONV_HALO, CONV_WIDTH), F32)

    c = c_ref[...]
    u = c[:, :CONV_WIDTH] * jax.nn.sigmoid(c[:, CONV_WIDTH:])
    u_ref[...] = u
    ext_ref[CONV_HALO:CONV_HALO + SEQ_TILE, :] = u
    span = SEQ_TILE + CONV_HALO - 8
    for ph in range(1, 8):
        sh_ref[ph, 0:span, :] = ext_ref[pl.ds(ph, span), :]

    def chunk(ci, carry):
        base = pl.multiple_of(ci * CONV_CHUNK, CONV_CHUNK)
        acc = jnp.zeros((CONV_CHUNK, CONV_WIDTH), F32)
        for j in range(CONV_KERNEL):
            off = j + 2
            a, ph = off // 8, off % 8
            if ph == 0:
                tap = ext_ref[pl.ds(base + 8 * a, CONV_CHUNK), :]
            else:
                tap = sh_ref[ph, pl.ds(base + 8 * a, CONV_CHUNK), :]
            acc = acc + tap * w_ref[j:j + 1, :]
        y = _layer_norm(acc + b_ref[...], g_ref[...], bb_ref[...])
        y_ref[pl.ds(base, CONV_CHUNK), :] = y * jax.nn.sigmoid(y)
        return carry

    lax.fori_loop(0, SEQ_TILE // CONV_CHUNK, chunk, 0)
    ext_ref[0:CONV_HALO, :] = u[SEQ_TILE - CONV_HALO:]


def _conv_prompt(c_in, conv_w, conv_b, ln_g, ln_b):
    nt = SEQ // SEQ_TILE
    vec = pl.BlockSpec((1, CONV_WIDTH), lambda b, t: (0, 0))
    return pl.pallas_call(
        _conv_prompt_kernel,
        grid=(BATCH, nt),
        in_specs=[pl.BlockSpec((SEQ_TILE, 2 * CONV_WIDTH), lambda b, t: (b * nt + t, 0)),
                  pl.BlockSpec((CONV_KERNEL, CONV_WIDTH), lambda b, t: (0, 0)),
                  vec, vec, vec],
        out_specs=[pl.BlockSpec((SEQ_TILE, CONV_WIDTH), lambda b, t: (b * nt + t, 0)),
                   pl.BlockSpec((SEQ_TILE, CONV_WIDTH), lambda b, t: (b * nt + t, 0))],
        out_shape=[jax.ShapeDtypeStruct((M_PROMPT, CONV_WIDTH), F32),
                   jax.ShapeDtypeStruct((M_PROMPT, CONV_WIDTH), F32)],
        scratch_shapes=[pltpu.VMEM((SEQ_TILE + CONV_HALO, CONV_WIDTH), F32),
                        pltpu.VMEM((8, SEQ_TILE + CONV_HALO, CONV_WIDTH), F32)],
        compiler_params=_params("arbitrary", "arbitrary"),
        name="conv_prompt",
    )(c_in, conv_w, conv_b, ln_g, ln_b)


def _attn_bias_tables():
    slopes = _alibi_slopes(ATTN_HEADS)
    qi = np.arange(ATTN_BLOCK)[:, None] + ATTN_BLOCK
    ki = np.arange(2 * ATTN_BLOCK)[None, :]
    delta = qi - ki
    tabs = np.zeros((len(DILATED), ATTN_HEADS // 2, 4, ATTN_BLOCK, 2 * ATTN_BLOCK), np.float32)
    for bi, (w, d) in enumerate(DILATED):
        n_back = w // d
        band = (delta >= 0) & (delta <= n_back)
        for h in range(ATTN_HEADS):
            bias = -slopes[h] * (delta * d).astype(np.float32)
            for first in range(2):
                valid = band & (ki >= ATTN_BLOCK) if first else band
                tabs[bi, h // 2, 2 * (h % 2) + first] = np.where(valid, bias, NEG_BIG)
    return tabs


def _attn_prompt_kernel(blocks_per_seq, q_ref, kp_ref, kc_ref, vp_ref, vc_ref, bias_ref, o_ref, lse_ref):
    t = pl.program_id(1)
    kk = jnp.concatenate([kp_ref[...], kc_ref[...]], axis=0)
    vv = jnp.concatenate([vp_ref[...], vc_ref[...]], axis=0)
    lane_q = lax.broadcasted_iota(jnp.int32, (ATTN_BLOCK, 2 * HEAD_DIM), 1)
    for j in range(ATTN_Q_TILE // ATTN_BLOCK):
        blk = t * (ATTN_Q_TILE // ATTN_BLOCK) + j
        first = (blk % blocks_per_seq == 0).astype(jnp.int32)
        q = q_ref[j * ATTN_BLOCK:(j + 1) * ATTN_BLOCK, :]
        keys = kk[j * ATTN_BLOCK:(j + 2) * ATTN_BLOCK]
        vals = vv[j * ATTN_BLOCK:(j + 2) * ATTN_BLOCK]
        outs, lses = [], []
        for h in range(2):
            in_head = (lane_q >= h * HEAD_DIM) & (lane_q < (h + 1) * HEAD_DIM)
            qh = jnp.where(in_head, q, jnp.zeros_like(q))
            s = lax.dot_general(qh, keys, (((1,), (1,)), ((), ())), preferred_element_type=F32)
            s = s * (HEAD_DIM ** -0.5) + bias_ref[2 * h + first]
            m = jnp.max(s, axis=-1, keepdims=True)
            p = jnp.exp(s - m)
            l = jnp.sum(p, axis=-1, keepdims=True)
            acc = jnp.dot(p.astype(BF16), vals, preferred_element_type=F32)
            outs.append(acc / l)
            lses.append(m + jnp.log(l))
        o_ref[j * ATTN_BLOCK:(j + 1) * ATTN_BLOCK, :] = jnp.where(lane_q < HEAD_DIM, outs[0], outs[1])
        lse_ref[j * ATTN_BLOCK:(j + 1) * ATTN_BLOCK, :] = jnp.where(lane_q < HEAD_DIM, lses[0], lses[1])


def _attn_prompt(qkv, bias_tab, dilation):
    n_hp = ATTN_HEADS // 2
    per_tile = ATTN_Q_TILE // ATTN_BLOCK
    nt = M_PROMPT // ATTN_Q_TILE
    width = 2 * HEAD_DIM
    cur = lambda off: pl.BlockSpec((ATTN_Q_TILE, width), lambda hp, t: (t, off * n_hp + hp))
    prev = lambda off: pl.BlockSpec((ATTN_BLOCK, width),
                                    lambda hp, t: (jnp.maximum(t * per_tile - 1, 0), off * n_hp + hp))
    out = pl.BlockSpec((ATTN_Q_TILE, width), lambda hp, t: (t, hp))
    return pl.pallas_call(
        functools.partial(_attn_prompt_kernel, SEQ // dilation // ATTN_BLOCK),
        grid=(n_hp, nt),
        in_specs=[cur(0), prev(1), cur(1), prev(2), cur(2),
                  pl.BlockSpec((None, 4, ATTN_BLOCK, 2 * ATTN_BLOCK), lambda hp, t: (hp, 0, 0, 0))],
        out_specs=[out, out],
        out_shape=[jax.ShapeDtypeStruct((M_PROMPT, ATTN_WIDTH), F32),
                   jax.ShapeDtypeStruct((M_PROMPT, ATTN_WIDTH), F32)],
        compiler_params=_params("arbitrary", "arbitrary"),
        name=f"attn_prompt_d{dilation}",
    )(qkv, qkv, qkv, qkv, qkv, bias_tab)


ATTN_CHUNKS = ATTN_WIDTH // LANES


def _attn_merge_kernel(o1_ref, l1_ref, o4_ref, l4_ref, o16_ref, l16_ref, y_ref, so_ref, sl_ref):
    for j, (o_ref, l_ref, d) in enumerate(((o4_ref, l4_ref, DILATED[1][1]), (o16_ref, l16_ref, DILATED[2][1]))):
        n = TOK_TILE // d
        for r in range(d):
            ov, lv = o_ref[r], l_ref[r]
            for c in range(ATTN_CHUNKS):
                so_ref[j, c, pl.ds(r, n, stride=d), :] = ov[:, c * LANES:(c + 1) * LANES]
                sl_ref[j, c, pl.ds(r, n, stride=d), :] = lv[:, c * LANES:(c + 1) * LANES]
    outs = [o1_ref[...]] + [jnp.concatenate([so_ref[j, c] for c in range(ATTN_CHUNKS)], axis=1) for j in range(2)]
    lses = [l1_ref[...]] + [jnp.concatenate([sl_ref[j, c] for c in range(ATTN_CHUNKS)], axis=1) for j in range(2)]
    top = jnp.maximum(jnp.maximum(lses[0], lses[1]), lses[2])
    ws = [jnp.exp(v - top) for v in lses]
    y_ref[...] = (ws[0] * outs[0] + ws[1] * outs[1] + ws[2] * outs[2]) / (ws[0] + ws[1] + ws[2])


def _attn_merge(branches):
    (o1, l1), (o4, l4), (o16, l16) = branches
    d4, d16 = DILATED[1][1], DILATED[2][1]
    nat = pl.BlockSpec((TOK_TILE, ATTN_WIDTH), lambda i: (i, 0))
    by_residue = lambda d: pl.BlockSpec((None, d, TOK_TILE // d, ATTN_WIDTH),
                                        lambda i: (i // TILES_PER_SEQ, 0, i % TILES_PER_SEQ, 0))
    view = lambda a, d: a.reshape(BATCH, d, SEQ // d, ATTN_WIDTH)
    return pl.pallas_call(
        _attn_merge_kernel,
        grid=(N_PROMPT_TILES,),
        in_specs=[nat, nat, by_residue(d4), by_residue(d4), by_residue(d16), by_residue(d16)],
        out_specs=nat,
        out_shape=jax.ShapeDtypeStruct((M_PROMPT, ATTN_WIDTH), F32),
        scratch_shapes=[pltpu.VMEM((2, ATTN_CHUNKS, TOK_TILE, LANES), F32),
                        pltpu.VMEM((2, ATTN_CHUNKS, TOK_TILE, LANES), F32)],
        compiler_params=_params("arbitrary"),
        name="attn_merge",
    )(o1, l1, view(o4, d4), view(l4, d4), view(o16, d16), view(l16, d16))


def _mix_decode_kernel(u_ref, c_ref, sp_ref, sc_ref, pw_ref, ps_ref, cw_ref, cb_ref, cg_ref, cbb_ref,
                       ya_ref, yc_ref, glu_ref):
    nb = DEC_BATCH
    u = u_ref[0:nb, :]
    lane = lax.broadcasted_iota(jnp.int32, (nb, POOL_WIDTH), 1)
    run = u
    sums = []
    for back in range(1, max(POOL_WINDOWS)):
        run = run + sp_ref[POOL_BUF - back]
        if back + 1 in POOL_WINDOWS:
            sums.append(run)
    wsel = _pool_select(lane, [jnp.full((nb, POOL_WIDTH), float(w), F32) for w in POOL_WINDOWS])
    d = _pool_select(lane, sums) / wsel - u
    ya_ref[...] = jnp.zeros_like(ya_ref)
    ya_ref[0:nb, :] = _bdot(d, pw_ref[...]) * ps_ref[...]

    c = c_ref[0:nb, :]
    glu = c[:, :CONV_WIDTH] * jax.nn.sigmoid(c[:, CONV_WIDTH:])
    glu_ref[...] = glu
    acc = glu * cw_ref[CONV_KERNEL - 1:CONV_KERNEL, :]
    for j in range(CONV_BUF):
        acc = acc + sc_ref[j] * cw_ref[j:j + 1, :]
    y = _layer_norm(acc + cb_ref[...], cg_ref[...], cbb_ref[...])
    yc_ref[...] = jnp.zeros_like(yc_ref)
    yc_ref[0:nb, :] = y * jax.nn.sigmoid(y)


def _mix_decode(u_pool, c_in, sp_t, sc_t, w_bd, scale, conv_w, conv_b, ln_g, ln_b):
    last = N_TOK_TILES - 1
    full = lambda shape: pl.BlockSpec(shape, lambda i: (0,) * len(shape))
    return pl.pallas_call(
        _mix_decode_kernel,
        grid=(1,),
        in_specs=[pl.BlockSpec((TOK_TILE, POOL_WIDTH), lambda i: (last, 0)),
                  pl.BlockSpec((TOK_TILE, 2 * CONV_WIDTH), lambda i: (last, 0)),
                  full((POOL_BUF, DEC_BATCH, POOL_WIDTH)), full((CONV_BUF, DEC_BATCH, CONV_WIDTH)),
                  full((POOL_WIDTH, POOL_WIDTH)), full((1, POOL_WIDTH)),
                  full((CONV_KERNEL, CONV_WIDTH)), full((1, CONV_WIDTH)), full((1, CONV_WIDTH)),
                  full((1, CONV_WIDTH))],
        out_specs=[full((TOK_TILE, POOL_WIDTH)), full((TOK_TILE, CONV_WIDTH)), full((DEC_BATCH, CONV_WIDTH))],
        out_shape=[jax.ShapeDtypeStruct((TOK_TILE, POOL_WIDTH), F32),
                   jax.ShapeDtypeStruct((TOK_TILE, CONV_WIDTH), F32),
                   jax.ShapeDtypeStruct((DEC_BATCH, CONV_WIDTH), F32)],
        compiler_params=_params("arbitrary"),
        name="mix_decode",
    )(u_pool, c_in, sp_t, sc_t, w_bd, scale, conv_w, conv_b, ln_g, ln_b)


def _attn_decode_tables():
    slopes = _alibi_slopes(ATTN_HEADS)
    pos = np.arange(CACHE_LEN)
    dist = (CACHE_LEN - pos).astype(np.float32)
    bias = np.zeros((8, CACHE_LEN), np.float32)
    for h in range(ATTN_HEADS):
        bias[h] = -slopes[h] * dist
    mult = np.zeros((1, CACHE_LEN), np.float32)
    for w, d in DILATED:
        mult[0] += ((CACHE_LEN - pos) % d == 0) & (CACHE_LEN - pos <= w)
    return bias, mult


def _round_bf16(x):
    return x.astype(BF16).astype(F32)


def _attn_decode_kernel(qkv_ref, k_ref, v_ref, bias_ref, mult_ref, y_ref):
    scale = HEAD_DIM ** -0.5
    mult = mult_ref[...]
    n_branch = float(len(DILATED))
    for h in range(ATTN_HEADS):
        q = _round_bf16(qkv_ref[0, h])
        k_new = _round_bf16(qkv_ref[1, h])
        v_new = _round_bf16(qkv_ref[2, h])
        s = jnp.sum(_round_bf16(k_ref[h]) * q, axis=0, keepdims=True) * scale + bias_ref[h:h + 1, :]
        s = jnp.where(mult > 0.0, s, NEG_BIG)
        s_self = jnp.sum(q * k_new, axis=0, keepdims=True) * scale
        m = jnp.maximum(jnp.max(s, axis=1, keepdims=True), s_self)
        p = mult * jnp.exp(s - m)
        p_self = n_branch * jnp.exp(s_self - m)
        l = jnp.sum(p, axis=1, keepdims=True) + p_self
        acc = jnp.sum(_round_bf16(v_ref[h]) * _round_bf16(p), axis=1, keepdims=True) + _round_bf16(p_self) * v_new
        y_ref[h] = acc / l


def _attn_decode(qkv_cols, cache_k_t, cache_v_t, bias, mult, layer):
    cache = pl.BlockSpec((None, None, ATTN_HEADS, HEAD_DIM, CACHE_LEN), lambda b: (layer, b, 0, 0, 0))
    return pl.pallas_call(
        _attn_decode_kernel,
        grid=(DEC_BATCH,),
        in_specs=[pl.BlockSpec((None, 3, ATTN_HEADS, HEAD_DIM, 1), lambda b: (b, 0, 0, 0, 0)),
                  cache, cache,
                  pl.BlockSpec((8, CACHE_LEN), lambda b: (0, 0)),
                  pl.BlockSpec((1, CACHE_LEN), lambda b: (0, 0))],
        out_specs=pl.BlockSpec((None, ATTN_HEADS, HEAD_DIM, 1), lambda b: (b, 0, 0, 0)),
        out_shape=jax.ShapeDtypeStruct((DEC_BATCH, ATTN_HEADS, HEAD_DIM, 1), F32),
        compiler_params=_params("arbitrary"),
        name="attn_decode",
    )(qkv_cols, cache_k_t, cache_v_t, bias, mult)


def _post_mix_kernel(x_ref, ya_ref, yb_ref, yc_ref, yad_ref, ybd_ref, ycd_ref, wo_ref, g_ref, b_ref, rw_ref, rb_ref,
                     tri_ref, x1_ref, info_ref, gate_ref, cnt_ref, wob_ref, run_ref):
    i = pl.program_id(0)

    @pl.when(i == 0)
    def _():
        wob_ref[...] = wo_ref[...].astype(BF16)
        run_ref[...] = jnp.zeros_like(run_ref)

    is_decode = i == N_TOK_TILES - 1
    mixed = jnp.concatenate([jnp.where(is_decode, d_ref[...], p_ref[...]) for p_ref, d_ref in
                             ((ya_ref, yad_ref), (yb_ref, ybd_ref), (yc_ref, ycd_ref))], axis=1)
    mix = jnp.dot(mixed.astype(BF16), wob_ref[...], preferred_element_type=F32)
    x1 = _layer_norm(DEEPNORM_ALPHA * x_ref[...] + mix, g_ref[...], b_ref[...])
    x1_ref[...] = x1
    logits = _bdot(x1, rw_ref[...]) + rb_ref[...]
    lane = lax.broadcasted_iota(jnp.int32, (TOK_TILE, LANES), 1).astype(F32)
    work = logits
    chosen = jnp.zeros((TOK_TILE, LANES), F32)
    tops, idxs, hots = [], [], []
    for _ in range(TOP_K):
        mk = jnp.max(work, axis=-1, keepdims=True)
        idx = jnp.min(jnp.where(work == mk, lane, float(LANES)), axis=-1, keepdims=True)
        hot = lane == idx
        tops.append(mk)
        idxs.append(idx)
        hots.append(hot)
        chosen = chosen + jnp.where(hot, 1.0, 0.0)
        work = jnp.where(hot, -jnp.inf, work)
    exps = [jnp.exp(v - tops[0]) for v in tops]
    denom = exps[0] + exps[1] + exps[2] + exps[3]
    ahead = jnp.dot(tri_ref[...], chosen.astype(BF16), preferred_element_type=F32) + run_ref[0:1, :]
    ranks = [jnp.sum(jnp.where(hot, ahead, 0.0), axis=-1, keepdims=True) for hot in hots]
    info = jnp.zeros((TOK_TILE, LANES), F32)
    gates = jnp.zeros((TOK_TILE, LANES), F32)
    for k in range(TOP_K):
        info = jnp.where(lane == float(k), idxs[k], info)
        info = jnp.where(lane == float(TOP_K + k), ranks[k], info)
        gates = jnp.where(lane == float(k), exps[k] / denom, gates)
    info_ref[...] = info.astype(jnp.int32)
    gate_ref[...] = gates
    run_ref[...] = run_ref[...] + jnp.sum(chosen, axis=0, keepdims=True)
    cnt_ref[...] = run_ref[...]


def _post_mix(x, mix_prompt, mix_decode, w_o, ln_g, ln_b, router_w, router_b, tri, layer):
    row = pl.BlockSpec((TOK_TILE, D_MODEL), lambda i: (i, 0))
    vec = pl.BlockSpec((1, D_MODEL), lambda i: (0, 0))
    lanes = pl.BlockSpec((TOK_TILE, LANES), lambda i: (i, 0))
    widths = (POOL_WIDTH, ATTN_WIDTH, CONV_WIDTH)
    prompt = [pl.BlockSpec((TOK_TILE, w), lambda i: (jnp.minimum(i, N_PROMPT_TILES - 1), 0)) for w in widths]
    decode = [pl.BlockSpec((TOK_TILE, w), lambda i: (0, 0)) for w in widths]
    return pl.pallas_call(
        _post_mix_kernel,
        grid=(N_TOK_TILES,),
        in_specs=[row] + prompt + decode + [
                  pl.BlockSpec((None, D_MODEL, D_MODEL), lambda i: (layer, 0, 0), pipeline_mode=pl.Buffered(1)),
                  vec, vec,
                  pl.BlockSpec((D_MODEL, LANES), lambda i: (0, 0)),
                  pl.BlockSpec((1, LANES), lambda i: (0, 0)),
                  pl.BlockSpec((TOK_TILE, TOK_TILE), lambda i: (0, 0))],
        out_specs=[row, lanes, lanes, pl.BlockSpec((8, LANES), lambda i: (0, 0))],
        out_shape=[jax.ShapeDtypeStruct((M_TOT, D_MODEL), F32),
                   jax.ShapeDtypeStruct((M_TOT, LANES), jnp.int32),
                   jax.ShapeDtypeStruct((M_TOT, LANES), F32),
                   jax.ShapeDtypeStruct((8, LANES), F32)],
        scratch_shapes=[pltpu.VMEM((D_MODEL, D_MODEL), BF16), pltpu.VMEM((8, LANES), F32)],
        compiler_params=_params("arbitrary"),
        name="post_mix",
    )(x, *mix_prompt, *mix_decode, w_o, ln_g, ln_b, router_w, router_b, tri)


def _dispatch_kernel(dest_ref, fill_ref, nu_ref, x_ref, xs_hbm, zero_ref, sem):
    i = pl.program_id(0)

    @pl.when(i == 0)
    def _():
        zero_ref[...] = jnp.zeros_like(zero_ref)

        def tile_copy(start):
            return pltpu.make_async_copy(zero_ref, xs_hbm.at[pl.ds(pl.multiple_of(start, EXPERT_TILE), EXPERT_TILE), :], sem)

        def over_partial_tiles(act):
            def body(e, carry):
                @pl.when(fill_ref[e] >= 0)
                def _():
                    act(tile_copy(fill_ref[e]))
                return carry
            lax.fori_loop(0, N_EXPERTS, body, 0)

        def over_unused_tiles(act):
            def body(tile, carry):
                act(tile_copy(tile * EXPERT_TILE))
                return carry
            lax.fori_loop(nu_ref[0], N_EXPERT_TILES, body, 0)

        over_partial_tiles(lambda c: c.start())
        over_unused_tiles(lambda c: c.start())
        over_partial_tiles(lambda c: c.wait())
        over_unused_tiles(lambda c: c.wait())

    def issue(r, carry):
        tok = i * TOK_TILE + r
        for k in range(TOP_K):
            pltpu.make_async_copy(x_ref.at[pl.ds(r, 1), :], xs_hbm.at[pl.ds(dest_ref[tok * TOP_K + k], 1), :],
                                  sem).start(priority=k % 2)
        return carry

    lax.fori_loop(0, TOK_TILE, issue, 0, unroll=8)
    for _ in range(TOP_K):
        pltpu.make_async_copy(x_ref, xs_hbm.at[pl.ds(0, TOK_TILE), :], sem).wait()


def _dispatch(dest_flat, fill_start, n_used, x1):
    return pl.pallas_call(
        _dispatch_kernel,
        grid_spec=pltpu.PrefetchScalarGridSpec(
            num_scalar_prefetch=3,
            grid=(N_TOK_TILES,),
            in_specs=[pl.BlockSpec((TOK_TILE, D_MODEL), lambda i, d, f, n: (i, 0))],
            out_specs=pl.BlockSpec(memory_space=pl.ANY),
            scratch_shapes=[pltpu.VMEM((EXPERT_TILE, D_MODEL), F32), pltpu.SemaphoreType.DMA]),
        out_shape=jax.ShapeDtypeStruct((N_SORTED, D_MODEL), F32),
        compiler_params=_params("arbitrary"),
        name="moe_dispatch",
    )(dest_flat, fill_start, n_used, x1)


def _experts_kernel(te_ref, nu_ref, x_ref, w1_ref, b1_ref, w2_ref, b2_ref, o_ref, w1b_ref, w2b_ref):
    i = pl.program_id(0)
    e = te_ref[i]
    e_prev = te_ref[jnp.maximum(i - 1, 0)]

    @pl.when((i == 0) | (e != e_prev))
    def _():
        w1b_ref[...] = w1_ref[...].astype(BF16)
        w2b_ref[...] = w2_ref[...].astype(BF16)

    @pl.when(i < nu_ref[0])
    def _():
        h = jnp.dot(x_ref[...].astype(BF16), w1b_ref[...], preferred_element_type=F32) + b1_ref[...]
        gate = jnp.minimum(h[:, :D_EXPERT], SWIGLU_LIMIT)
        up = jnp.clip(h[:, D_EXPERT:], -SWIGLU_LIMIT, SWIGLU_LIMIT)
        act = gate * jax.nn.sigmoid(SWIGLU_ALPHA * gate) * (up + 1.0)
        o_ref[...] = jnp.dot(act.astype(BF16), w2b_ref[...], preferred_element_type=F32) + b2_ref[...]

    @pl.when(i >= nu_ref[0])
    def _():
        o_ref[...] = jnp.zeros_like(o_ref)


def _experts(tile_expert, n_used, xs, w1, b1, w2, b2, layer):
    by_expert = lambda *shape: pl.BlockSpec((None, None) + shape, lambda i, te, nu: (layer, te[i], 0, 0))
    rows = pl.BlockSpec((EXPERT_TILE, D_MODEL), lambda i, te, nu: (jnp.minimum(i, nu[0] - 1), 0))
    return pl.pallas_call(
        _experts_kernel,
        grid_spec=pltpu.PrefetchScalarGridSpec(
            num_scalar_prefetch=2,
            grid=(N_EXPERT_TILES,),
            in_specs=[rows, by_expert(D_MODEL, 2 * D_EXPERT), by_expert(1, 2 * D_EXPERT),
                      by_expert(D_EXPERT, D_MODEL), by_expert(1, D_MODEL)],
            out_specs=pl.BlockSpec((EXPERT_TILE, D_MODEL), lambda i, te, nu: (i, 0)),
            scratch_shapes=[pltpu.VMEM((D_MODEL, 2 * D_EXPERT), BF16), pltpu.VMEM((D_EXPERT, D_MODEL), BF16)]),
        out_shape=jax.ShapeDtypeStruct((N_SORTED, D_MODEL), F32),
        compiler_params=_params("arbitrary"),
        name="moe_experts",
    )(tile_expert, n_used, xs, w1, b1, w2, b2)


def _combine_kernel(dest_ref, x1_ref, gate_ref, g_ref, b_ref, ys_hbm, o_ref, rows_ref, sem):
    i = pl.program_id(0)

    def gather(tile, slot):
        def issue(r, carry):
            tok = tile * TOK_TILE + r
            for k in range(TOP_K):
                pltpu.make_async_copy(ys_hbm.at[pl.ds(dest_ref[tok * TOP_K + k], 1), :],
                                      rows_ref.at[slot, k, pl.ds(r, 1), :], sem.at[slot]).start(priority=k % 2)
            return carry
        lax.fori_loop(0, TOK_TILE, issue, 0, unroll=8)

    @pl.when(i == 0)
    def _():
        gather(0, 0)

    @pl.when(i + 1 < N_TOK_TILES)
    def _():
        gather(i + 1, (i + 1) % 2)

    slot = i % 2
    for k in range(TOP_K):
        pltpu.make_async_copy(ys_hbm.at[pl.ds(0, TOK_TILE), :], rows_ref.at[slot, k], sem.at[slot]).wait()
    gates = gate_ref[...]
    moe = gates[:, 0:1] * rows_ref[slot, 0]
    for k in range(1, TOP_K):
        moe = moe + gates[:, k:k + 1] * rows_ref[slot, k]
    o_ref[...] = _layer_norm(DEEPNORM_ALPHA * x1_ref[...] + moe, g_ref[...], b_ref[...])


def _combine(dest_flat, x1, gates, ln_g, ln_b, ys):
    row = pl.BlockSpec((TOK_TILE, D_MODEL), lambda i, d: (i, 0))
    vec = pl.BlockSpec((1, D_MODEL), lambda i, d: (0, 0))
    return pl.pallas_call(
        _combine_kernel,
        grid_spec=pltpu.PrefetchScalarGridSpec(
            num_scalar_prefetch=1,
            grid=(N_TOK_TILES,),
            in_specs=[row, pl.BlockSpec((TOK_TILE, LANES), lambda i, d: (i, 0)), vec, vec,
                      pl.BlockSpec(memory_space=pl.ANY)],
            out_specs=row,
            scratch_shapes=[pltpu.VMEM((2, TOP_K, TOK_TILE, D_MODEL), F32), pltpu.SemaphoreType.DMA((2,))]),
        out_shape=jax.ShapeDtypeStruct((M_TOT, D_MODEL), F32),
        compiler_params=_params("arbitrary"),
        name="moe_combine",
    )(dest_flat, x1, gates, ln_g, ln_b, ys)


def _routing_tables(info, counts):
    top_i = info[:, :TOP_K]
    rank = info[:, TOP_K:2 * TOP_K]
    cnt = counts[0, :N_EXPERTS].astype(jnp.int32)
    padded = (cnt + EXPERT_TILE - 1) // EXPERT_TILE * EXPERT_TILE
    pad_end = jnp.cumsum(padded)
    pad_start = pad_end - padded
    hot = top_i[:, :, None] == jnp.arange(N_EXPERTS, dtype=jnp.int32)
    dest = rank + jnp.sum(jnp.where(hot, pad_start, 0), axis=-1)
    fill_start = jnp.where(cnt % EXPERT_TILE != 0, pad_end - EXPERT_TILE, -1).astype(jnp.int32)
    tile_start = jnp.arange(N_EXPERT_TILES, dtype=jnp.int32) * EXPERT_TILE
    n_used = (pad_end[-1] // EXPERT_TILE).astype(jnp.int32)
    tile_expert = jnp.sum(tile_start[:, None] >= pad_end[None, :], axis=1).astype(jnp.int32)
    last_expert = tile_expert[jnp.maximum(n_used - 1, 0)]
    tile_expert = jnp.where(tile_start < pad_end[-1], tile_expert, last_expert)
    return dest.reshape(-1).astype(jnp.int32), fill_start, tile_expert, n_used.reshape(1)


def _block_diag(pool_w):
    g, n, _ = pool_w.shape
    out = jnp.zeros((g * n, g * n), pool_w.dtype)
    for i in range(g):
        out = out.at[i * n:(i + 1) * n, i * n:(i + 1) * n].set(pool_w[i])
    return out


def kernel(x_prompt, x_sample, state_pool, state_conv, cache_attn_k, cache_attn_v, w_in, pool_w, pool_scale, conv_w, conv_b, conv_ln_g, conv_ln_b, w_o, ln1_g, ln1_b, router_w, router_b, expert_w_in, expert_b_in, expert_w_out, expert_b_out, ln2_g, ln2_b):
    bias_prompt = jnp.asarray(_attn_bias_tables())
    bias_decode, mult_decode = (jnp.asarray(t) for t in _attn_decode_tables())
    tri = jnp.asarray(np.tril(np.ones((TOK_TILE, TOK_TILE), np.float32), -1), BF16)
    x = jnp.concatenate([x_prompt.reshape(M_PROMPT, D_MODEL), x_sample.reshape(DEC_BATCH, D_MODEL),
                         jnp.zeros((M_TOT - M_PROMPT - DEC_BATCH, D_MODEL), F32)], axis=0)
    cache_k_t = cache_attn_k.transpose(0, 1, 3, 4, 2)
    cache_v_t = cache_attn_v.transpose(0, 1, 3, 4, 2)
    b_in = expert_b_in.reshape(DEPTH, N_EXPERTS, 1, 2 * D_EXPERT)
    b_out = expert_b_out.reshape(DEPTH, N_EXPERTS, 1, D_MODEL)
    keep = min(CACHE_LEN, SEQ)
    outs = {name: [] for name in ("p_pool", "p_conv", "p_k", "p_v", "s_pool", "s_conv", "s_k", "s_v")}
    for l in range(DEPTH):
        u_pool, qkv, c_in, q1, q4, q16 = _in_proj(x, w_in, l)
        w_bd = _block_diag(pool_w[l])
        scale = pool_scale[l].reshape(1, POOL_WIDTH)
        cb, cg, cbb = (v[l].reshape(1, CONV_WIDTH) for v in (conv_b, conv_ln_g, conv_ln_b))

        ya = _pool_prompt(u_pool, w_bd, scale)
        yc, glu = _conv_prompt(c_in, conv_w[l], cb, cg, cbb)
        branches = [_attn_prompt(q.reshape(M_PROMPT, QKV_WIDTH), bias_prompt[bi], d)
                    for bi, (q, (_, d)) in enumerate(zip((q1, q4, q16), DILATED))]
        yb = _attn_merge(branches)

        sp_t = state_pool[l].transpose(1, 0, 2)
        sc_t = state_conv[l].transpose(1, 0, 2)
        ya_d, yc_d, glu_d = _mix_decode(u_pool, c_in, sp_t, sc_t, w_bd, scale, conv_w[l], cb, cg, cbb)
        qkv_d = qkv[M_PROMPT:M_PROMPT + DEC_BATCH]
        yb_d = _attn_decode(qkv_d.reshape(DEC_BATCH, 3, ATTN_HEADS, HEAD_DIM, 1), cache_k_t, cache_v_t,
                            bias_decode, mult_decode, l)
        yb_d = jnp.pad(yb_d.reshape(DEC_BATCH, ATTN_WIDTH), ((0, TOK_TILE - DEC_BATCH), (0, 0)))

        rw = jnp.pad(router_w[l], ((0, 0), (0, LANES - N_EXPERTS)))
        rb = jnp.pad(router_b[l], (0, LANES - N_EXPERTS), constant_values=NEG_BIG).reshape(1, LANES)
        x1, info, gates, counts = _post_mix(x, (ya, yb, yc), (ya_d, yb_d, yc_d), w_o, ln1_g[l].reshape(1, D_MODEL),
                                            ln1_b[l].reshape(1, D_MODEL), rw, rb, tri, l)

        dest, fill_start, tile_expert, n_used = _routing_tables(info, counts)
        xs = _dispatch(dest, fill_start, n_used, x1)
        ys = _experts(tile_expert, n_used, xs, expert_w_in, b_in, expert_w_out, b_out, l)
        x = _combine(dest, x1, gates, ln2_g[l].reshape(1, D_MODEL), ln2_b[l].reshape(1, D_MODEL), ys)

        qkv_p = qkv[:M_PROMPT]
        k_p = qkv_p[:, ATTN_WIDTH:2 * ATTN_WIDTH].reshape(BATCH, SEQ, ATTN_HEADS, HEAD_DIM)
        v_p = qkv_p[:, 2 * ATTN_WIDTH:].reshape(BATCH, SEQ, ATTN_HEADS, HEAD_DIM)
        outs["p_pool"].append(u_pool[:M_PROMPT].reshape(BATCH, SEQ, POOL_WIDTH)[:, -POOL_BUF:])
        outs["p_conv"].append(glu.reshape(BATCH, SEQ, CONV_WIDTH)[:, -CONV_BUF:])
        outs["p_k"].append(k_p[:, -keep:])
        outs["p_v"].append(v_p[:, -keep:])
        u_d = u_pool[M_PROMPT:M_PROMPT + DEC_BATCH]
        outs["s_pool"].append(jnp.concatenate([state_pool[l][:, 1:], u_d[:, None]], axis=1))
        outs["s_conv"].append(jnp.concatenate([state_conv[l][:, 1:], glu_d[:, None]], axis=1))
        outs["s_k"].append(qkv_d[:, ATTN_WIDTH:2 * ATTN_WIDTH].reshape(DEC_BATCH, 1, ATTN_HEADS, HEAD_DIM))
        outs["s_v"].append(qkv_d[:, 2 * ATTN_WIDTH:].reshape(DEC_BATCH, 1, ATTN_HEADS, HEAD_DIM))

    y_p = x[:M_PROMPT].reshape(BATCH, SEQ, D_MODEL)
    y_s = x[M_PROMPT:M_PROMPT + DEC_BATCH].reshape(DEC_BATCH, 1, D_MODEL)
    return (y_p, y_s) + tuple(jnp.stack(outs[name]) for name in
                              ("p_pool", "p_conv", "p_k", "p_v", "s_pool", "s_conv", "s_k", "s_v"))
```

```python
import functools
import math

import numpy as np
import jax
import jax.numpy as jnp
from jax import lax
from jax.experimental import pallas as pl
from jax.experimental.pallas import tpu as pltpu

F32 = jnp.float32
BF16 = jnp.bfloat16

D_MODEL = 1024
BATCH = 4
SEQ = 4096
DEPTH = 4
DEC_BATCH = 32
PAST_LEN = 8192
POOL_WIDTH = 256
POOL_WINDOWS = (2, 4, 8, 16)
POOL_BUF = 15
ATTN_HEADS = 6
HEAD_DIM = 64
ATTN_WIDTH = ATTN_HEADS * HEAD_DIM
DILATED = ((128, 1), (512, 4), (2048, 16))
ATTN_BLOCK = 128
CACHE_LEN = 2048
CONV_WIDTH = 384
CONV_KERNEL = 31
CONV_BUF = 30
N_EXPERTS = 32
TOP_K = 4
D_EXPERT = 1024
SWIGLU_ALPHA = 1.702
SWIGLU_LIMIT = 7.0
DEEPNORM_ALPHA = (2 * DEPTH) ** 0.25
LN_EPS = 1e-5
NEG_BIG = -1e30
LANES = 128

M_PROMPT = BATCH * SEQ
TOK_TILE = 256
M_TOT = M_PROMPT + TOK_TILE
N_TOK_TILES = M_TOT // TOK_TILE
SEQ_TILE = 512
ATTN_Q_TILE = 512
EXPERT_TILE = 512
N_ASSIGN = M_TOT * TOP_K
N_EXPERT_TILES = -(-N_ASSIGN // EXPERT_TILE) + N_EXPERTS
N_SORTED = N_EXPERT_TILES * EXPERT_TILE
VMEM_LIMIT = 56 * 1024 * 1024


def _alibi_slopes(n):
    def pow2(k):
        start = 2.0 ** (-8.0 / k)
        return [start ** (i + 1) for i in range(k)]
    if math.log2(n).is_integer():
        return pow2(n)
    c = 2 ** math.floor(math.log2(n))
    return pow2(c) + pow2(2 * c)[0::2][: n - c]


def _bdot(a, b):
    return jnp.dot(a.astype(BF16), b.astype(BF16), preferred_element_type=F32)


def _layer_norm(x, g, b):
    mu = jnp.mean(x, axis=-1, keepdims=True)
    xc = x - mu
    var = jnp.mean(xc * xc, axis=-1, keepdims=True)
    return xc * lax.rsqrt(var + LN_EPS) * g + b


def _params(*sem):
    return pltpu.CompilerParams(dimension_semantics=sem, vmem_limit_bytes=VMEM_LIMIT)


N_PROMPT_TILES = M_PROMPT // TOK_TILE
TILES_PER_SEQ = SEQ // TOK_TILE
QKV_WIDTH = 3 * ATTN_WIDTH
QKV_CHUNKS = QKV_WIDTH // LANES


def _in_proj_kernel(x_ref, w_ref, pool_ref, conv_ref, qkvd_ref, q1_ref, q4_ref, q16_ref, kvt_ref, wb_ref, stage_ref):
    @pl.when(pl.program_id(0) == 0)
    def _():
        wb_ref[...] = w_ref[...].astype(BF16)

    res = jnp.dot(x_ref[...].astype(BF16), wb_ref[...], preferred_element_type=F32)
    qkv = res[:, POOL_WIDTH:POOL_WIDTH + QKV_WIDTH]
    pool_ref[...] = res[:, :POOL_WIDTH]
    conv_ref[...] = res[:, POOL_WIDTH + QKV_WIDTH:]
    qkvd_ref[...] = qkv
    kvt_ref[...] = jnp.transpose(qkv[:, ATTN_WIDTH:])
    q1_ref[...] = qkv.astype(BF16)
    for c in range(QKV_CHUNKS):
        stage_ref[c] = qkv[:, c * LANES:(c + 1) * LANES]
    for ref, d in ((q4_ref, DILATED[1][1]), (q16_ref, DILATED[2][1])):
        n = TOK_TILE // d
        for r in range(d):
            rows = [stage_ref[c, pl.ds(r, n, stride=d), :] for c in range(QKV_CHUNKS)]
            ref[r] = jnp.concatenate(rows, axis=1).astype(BF16)


CACHE_TILE0 = (SEQ - CACHE_LEN) // TOK_TILE


def _in_proj(x, w_in, layer):
    n_in = w_in.shape[2]
    d4, d16 = DILATED[1][1], DILATED[2][1]
    by_residue = lambda d: pl.BlockSpec(
        (None, d, TOK_TILE // d, QKV_WIDTH), lambda i: (i // TILES_PER_SEQ, 0, i % TILES_PER_SEQ, 0))
    return pl.pallas_call(
        _in_proj_kernel,
        grid=(N_TOK_TILES,),
        in_specs=[pl.BlockSpec((TOK_TILE, D_MODEL), lambda i: (i, 0)),
                  pl.BlockSpec((None, D_MODEL, n_in), lambda i: (layer, 0, 0), pipeline_mode=pl.Buffered(1))],
        out_specs=[pl.BlockSpec((TOK_TILE, POOL_WIDTH), lambda i: (i, 0)),
                   pl.BlockSpec((TOK_TILE, 2 * CONV_WIDTH), lambda i: (i, 0)),
                   pl.BlockSpec((TOK_TILE, QKV_WIDTH), lambda i: (0, 0)),
                   pl.BlockSpec((TOK_TILE, QKV_WIDTH), lambda i: (i, 0)),
                   by_residue(d4), by_residue(d16),
                   pl.BlockSpec((None, 2 * ATTN_WIDTH, TOK_TILE),
                                lambda i: (i // TILES_PER_SEQ, 0, jnp.maximum(i % TILES_PER_SEQ - CACHE_TILE0, 0)))],
        out_shape=[jax.ShapeDtypeStruct((M_TOT, POOL_WIDTH), F32),
                   jax.ShapeDtypeStruct((M_TOT, 2 * CONV_WIDTH), F32),
                   jax.ShapeDtypeStruct((TOK_TILE, QKV_WIDTH), F32),
                   jax.ShapeDtypeStruct((M_TOT, QKV_WIDTH), BF16),
                   jax.ShapeDtypeStruct((BATCH + 1, d4, SEQ // d4, QKV_WIDTH), BF16),
                   jax.ShapeDtypeStruct((BATCH + 1, d16, SEQ // d16, QKV_WIDTH), BF16),
                   jax.ShapeDtypeStruct((BATCH + 1, 2 * ATTN_WIDTH, CACHE_LEN), F32)],
        scratch_shapes=[pltpu.VMEM((D_MODEL, n_in), BF16), pltpu.VMEM((QKV_CHUNKS, TOK_TILE, LANES), F32)],
        compiler_params=_params("arbitrary"),
        name="in_proj",
    )(x, w_in)


def _pool_select(lane, vals):
    out = vals[-1]
    for g in range(len(vals) - 2, -1, -1):
        out = jnp.where(lane < (g + 1) * 64, vals[g], out)
    return out


def _pool_prompt_kernel(u_ref, w_ref, scale_ref, o_ref, halo_ref):
    t = pl.program_id(1)

    @pl.when(t == 0)
    def _():
        halo_ref[...] = jnp.zeros_like(halo_ref)

    u = u_ref[...]
    n = SEQ_TILE + 16
    ext = jnp.concatenate([halo_ref[...], u], axis=0)
    p2 = ext + pltpu.roll(ext, 1, 0)
    p4 = p2 + pltpu.roll(p2, 2, 0)
    p8 = p4 + pltpu.roll(p4, 4, 0)
    p16 = p8 + pltpu.roll(p8, 8, 0)
    lane = lax.broadcasted_iota(jnp.int32, (SEQ_TILE, POOL_WIDTH), 1)
    row = lax.broadcasted_iota(jnp.int32, (SEQ_TILE, POOL_WIDTH), 0)
    pos1 = (row + t * SEQ_TILE + 1).astype(F32)
    wsel = _pool_select(lane, [jnp.full((SEQ_TILE, POOL_WIDTH), float(w), F32) for w in POOL_WINDOWS])
    cnt = jnp.minimum(wsel, pos1)
    sums = _pool_select(lane, [p[16:n] for p in (p2, p4, p8, p16)])
    d = sums / cnt - u
    o_ref[...] = _bdot(d, w_ref[...]) * scale_ref[...]
    halo_ref[...] = u[SEQ_TILE - 16:]


def _pool_prompt(u_pool, w_bd, scale):
    nt = SEQ // SEQ_TILE
    return pl.pallas_call(
        _pool_prompt_kernel,
        grid=(BATCH, nt),
        in_specs=[pl.BlockSpec((SEQ_TILE, POOL_WIDTH), lambda b, t: (b * nt + t, 0)),
                  pl.BlockSpec((POOL_WIDTH, POOL_WIDTH), lambda b, t: (0, 0)),
                  pl.BlockSpec((1, POOL_WIDTH), lambda b, t: (0, 0))],
        out_specs=pl.BlockSpec((SEQ_TILE, POOL_WIDTH), lambda b, t: (b * nt + t, 0)),
        out_shape=jax.ShapeDtypeStruct((M_PROMPT, POOL_WIDTH), F32),
        scratch_shapes=[pltpu.VMEM((16, POOL_WIDTH), F32)],
        compiler_params=_params("arbitrary", "arbitrary"),
        name="pool_prompt",
    )(u_pool, w_bd, scale)


CONV_HALO = 32
CONV_CHUNK = 64


def _conv_prompt_kernel(c_ref, w_ref, b_ref, g_ref, bb_ref, y_ref, u_ref, ext_ref, sh_ref):
    t = pl.program_id(1)

    @pl.when(t == 0)
    def _():
        ext_ref[0:CONV_HALO, :] = jnp.zeros((CONV_HALO, CONV_WIDTH), F32)

    c = c_ref[...]
    u = c[:, :CONV_WIDTH] * jax.nn.sigmoid(c[:, CONV_WIDTH:])
    u_ref[...] = u
    ext_ref[CONV_HALO:CONV_HALO + SEQ_TILE, :] = u
    span = SEQ_TILE + CONV_HALO - 8
    for ph in range(1, 8):
        sh_ref[ph, 0:span, :] = ext_ref[pl.ds(ph, span), :]

    def chunk(ci, carry):
        base = pl.multiple_of(ci * CONV_CHUNK, CONV_CHUNK)
        acc = jnp.zeros((CONV_CHUNK, CONV_WIDTH), F32)
        for j in range(CONV_KERNEL):
            off = j + 2
            a, ph = off // 8, off % 8
            if ph == 0:
                tap = ext_ref[pl.ds(base + 8 * a, CONV_CHUNK), :]
            else:
                tap = sh_ref[ph, pl.ds(base + 8 * a, CONV_CHUNK), :]
            acc = acc + tap * w_ref[j:j + 1, :]
        y = _layer_norm(acc + b_ref[...], g_ref[...], bb_ref[...])
        y_ref[pl.ds(base, CONV_CHUNK), :] = y * jax.nn.sigmoid(y)
        return carry

    lax.fori_loop(0, SEQ_TILE // CONV_CHUNK, chunk, 0)
    ext_ref[0:CONV_HALO, :] = u[SEQ_TILE - CONV_HALO:]


def _conv_prompt(c_in, conv_w, conv_b, ln_g, ln_b):
    nt = SEQ // SEQ_TILE
    vec = pl.BlockSpec((1, CONV_WIDTH), lambda b, t: (0, 0))
    return pl.pallas_call(
        _conv_prompt_kernel,
        grid=(BATCH, nt),
        in_specs=[pl.BlockSpec((SEQ_TILE, 2 * CONV_WIDTH), lambda b, t: (b * nt + t, 0)),
                  pl.BlockSpec((CONV_KERNEL, CONV_WIDTH), lambda b, t: (0, 0)),
                  vec, vec, vec],
        out_specs=[pl.BlockSpec((SEQ_TILE, CONV_WIDTH), lambda b, t: (b * nt + t, 0)),
                   pl.BlockSpec((SEQ_TILE, CONV_WIDTH), lambda b, t: (b * nt + t, 0))],
        out_shape=[jax.ShapeDtypeStruct((M_PROMPT, CONV_WIDTH), F32),
                   jax.ShapeDtypeStruct((M_PROMPT, CONV_WIDTH), F32)],
        scratch_shapes=[pltpu.VMEM((SEQ_TILE + CONV_HALO, CONV_WIDTH), F32),
                        pltpu.VMEM((8, SEQ_TILE + CONV_HALO, CONV_WIDTH), F32)],
        compiler_params=_params("arbitrary", "arbitrary"),
        name="conv_prompt",
    )(c_in, conv_w, conv_b, ln_g, ln_b)


def _attn_bias_tables():
    slopes = _alibi_slopes(ATTN_HEADS)
    qi = np.arange(ATTN_BLOCK)[:, None] + ATTN_BLOCK
    ki = np.arange(2 * ATTN_BLOCK)[None, :]
    delta = qi - ki
    tabs = np.zeros((len(DILATED), ATTN_HEADS // 2, 4, ATTN_BLOCK, 2 * ATTN_BLOCK), np.float32)
    for bi, (w, d) in enumerate(DILATED):
        n_back = w // d
        band = (delta >= 0) & (delta <= n_back)
        for h in range(ATTN_HEADS):
            bias = -slopes[h] * (delta * d).astype(np.float32)
            for first in range(2):
                valid = band & (ki >= ATTN_BLOCK) if first else band
                tabs[bi, h // 2, 2 * (h % 2) + first] = np.where(valid, bias, NEG_BIG)
    return tabs


def _attn_prompt_kernel(blocks_per_seq, q_ref, kp_ref, kc_ref, vp_ref, vc_ref, bias_ref, o_ref, lse_ref):
    t = pl.program_id(1)
    kk = jnp.concatenate([kp_ref[...], kc_ref[...]], axis=0)
    vv = jnp.concatenate([vp_ref[...], vc_ref[...]], axis=0)
    lane_q = lax.broadcasted_iota(jnp.int32, (ATTN_BLOCK, 2 * HEAD_DIM), 1)
    for j in range(ATTN_Q_TILE // ATTN_BLOCK):
        blk = t * (ATTN_Q_TILE // ATTN_BLOCK) + j
        first = (blk % blocks_per_seq == 0).astype(jnp.int32)
        q = q_ref[j * ATTN_BLOCK:(j + 1) * ATTN_BLOCK, :]
        keys = kk[j * ATTN_BLOCK:(j + 2) * ATTN_BLOCK]
        vals = vv[j * ATTN_BLOCK:(j + 2) * ATTN_BLOCK]
        outs, lses = [], []
        for h in range(2):
            in_head = (lane_q >= h * HEAD_DIM) & (lane_q < (h + 1) * HEAD_DIM)
            qh = jnp.where(in_head, q, jnp.zeros_like(q))
            s = lax.dot_general(qh, keys, (((1,), (1,)), ((), ())), preferred_element_type=F32)
            s = s * (HEAD_DIM ** -0.5) + bias_ref[2 * h + first]
            m = jnp.max(s, axis=-1, keepdims=True)
            p = jnp.exp(s - m)
            l = jnp.sum(p, axis=-1, keepdims=True)
            acc = jnp.dot(p.astype(BF16), vals, preferred_element_type=F32)
            outs.append(acc / l)
            lses.append(m + jnp.log(l))
        o_ref[j * ATTN_BLOCK:(j + 1) * ATTN_BLOCK, :] = jnp.where(lane_q < HEAD_DIM, outs[0], outs[1])
        lse_ref[j * ATTN_BLOCK:(j + 1) * ATTN_BLOCK, :] = jnp.where(lane_q < HEAD_DIM, lses[0], lses[1])


def _attn_prompt(qkv, bias_tab, dilation):
    n_hp = ATTN_HEADS // 2
    per_tile = ATTN_Q_TILE // ATTN_BLOCK
    nt = M_PROMPT // ATTN_Q_TILE
    width = 2 * HEAD_DIM
    cur = lambda off: pl.BlockSpec((ATTN_Q_TILE, width), lambda hp, t: (t, off * n_hp + hp))
    prev = lambda off: pl.BlockSpec((ATTN_BLOCK, width),
                                    lambda hp, t: (jnp.maximum(t * per_tile - 1, 0), off * n_hp + hp))
    out = pl.BlockSpec((ATTN_Q_TILE, width), lambda hp, t: (t, hp))
    return pl.pallas_call(
        functools.partial(_attn_prompt_kernel, SEQ // dilation // ATTN_BLOCK),
        grid=(n_hp, nt),
        in_specs=[cur(0), prev(1), cur(1), prev(2), cur(2),
                  pl.BlockSpec((None, 4, ATTN_BLOCK, 2 * ATTN_BLOCK), lambda hp, t: (hp, 0, 0, 0))],
        out_specs=[out, out],
        out_shape=[jax.ShapeDtypeStruct((M_PROMPT, ATTN_WIDTH), F32),
                   jax.ShapeDtypeStruct((M_PROMPT, ATTN_WIDTH), F32)],
        compiler_params=_params("arbitrary", "arbitrary"),
        name=f"attn_prompt_d{dilation}",
    )(qkv, qkv, qkv, qkv, qkv, bias_tab)


ATTN_CHUNKS = ATTN_WIDTH // LANES


def _attn_merge_kernel(o1_ref, l1_ref, o4_ref, l4_ref, o16_ref, l16_ref, y_ref, so_ref, sl_ref):
    for j, (o_ref, l_ref, d) in enumerate(((o4_ref, l4_ref, DILATED[1][1]), (o16_ref, l16_ref, DILATED[2][1]))):
        n = TOK_TILE // d
        for r in range(d):
            ov, lv = o_ref[r], l_ref[r]
            for c in range(ATTN_CHUNKS):
                so_ref[j, c, pl.ds(r, n, stride=d), :] = ov[:, c * LANES:(c + 1) * LANES]
                sl_ref[j, c, pl.ds(r, n, stride=d), :] = lv[:, c * LANES:(c + 1) * LANES]
    outs = [o1_ref[...]] + [jnp.concatenate([so_ref[j, c] for c in range(ATTN_CHUNKS)], axis=1) for j in range(2)]
    lses = [l1_ref[...]] + [jnp.concatenate([sl_ref[j, c] for c in range(ATTN_CHUNKS)], axis=1) for j in range(2)]
    top = jnp.maximum(jnp.maximum(lses[0], lses[1]), lses[2])
    ws = [jnp.exp(v - top) for v in lses]
    y_ref[...] = (ws[0] * outs[0] + ws[1] * outs[1] + ws[2] * outs[2]) / (ws[0] + ws[1] + ws[2])


def _attn_merge(branches):
    (o1, l1), (o4, l4), (o16, l16) = branches
    d4, d16 = DILATED[1][1], DILATED[2][1]
    nat = pl.BlockSpec((TOK_TILE, ATTN_WIDTH), lambda i: (i, 0))
    by_residue = lambda d: pl.BlockSpec((None, d, TOK_TILE // d, ATTN_WIDTH),
                                        lambda i: (i // TILES_PER_SEQ, 0, i % TILES_PER_SEQ, 0))
    view = lambda a, d: a.reshape(BATCH, d, SEQ // d, ATTN_WIDTH)
    return pl.pallas_call(
        _attn_merge_kernel,
        grid=(N_PROMPT_TILES,),
        in_specs=[nat, nat, by_residue(d4), by_residue(d4), by_residue(d16), by_residue(d16)],
        out_specs=nat,
        out_shape=jax.ShapeDtypeStruct((M_PROMPT, ATTN_WIDTH), F32),
        scratch_shapes=[pltpu.VMEM((2, ATTN_CHUNKS, TOK_TILE, LANES), F32),
                        pltpu.VMEM((2, ATTN_CHUNKS, TOK_TILE, LANES), F32)],
        compiler_params=_params("arbitrary"),
        name="attn_merge",
    )(o1, l1, view(o4, d4), view(l4, d4), view(o16, d16), view(l16, d16))


def _mix_decode_kernel(u_ref, c_ref, sp_ref, sc_ref, pw_ref, ps_ref, cw_ref, cb_ref, cg_ref, cbb_ref,
                       ya_ref, yc_ref, glu_ref):
    nb = DEC_BATCH
    u = u_ref[0:nb, :]
    lane = lax.broadcasted_iota(jnp.int32, (nb, POOL_WIDTH), 1)
    run = u
    sums = []
    for back in range(1, max(POOL_WINDOWS)):
        run = run + sp_ref[POOL_BUF - back]
        if back + 1 in POOL_WINDOWS:
            sums.append(run)
    wsel = _pool_select(lane, [jnp.full((nb, POOL_WIDTH), float(w), F32) for w in POOL_WINDOWS])
    d = _pool_select(lane, sums) / wsel - u
    ya_ref[...] = jnp.zeros_like(ya_ref)
    ya_ref[0:nb, :] = _bdot(d, pw_ref[...]) * ps_ref[...]

    c = c_ref[0:nb, :]
    glu = c[:, :CONV_WIDTH] * jax.nn.sigmoid(c[:, CONV_WIDTH:])
    glu_ref[...] = glu
    acc = glu * cw_ref[CONV_KERNEL - 1:CONV_KERNEL, :]
    for j in range(CONV_BUF):
        acc = acc + sc_ref[j] * cw_ref[j:j + 1, :]
    y = _layer_norm(acc + cb_ref[...], cg_ref[...], cbb_ref[...])
    yc_ref[...] = jnp.zeros_like(yc_ref)
    yc_ref[0:nb, :] = y * jax.nn.sigmoid(y)


def _mix_decode(u_pool, c_in, sp_t, sc_t, w_bd, scale, conv_w, conv_b, ln_g, ln_b):
    last = N_TOK_TILES - 1
    full = lambda shape: pl.BlockSpec(shape, lambda i: (0,) * len(shape))
    return pl.pallas_call(
        _mix_decode_kernel,
        grid=(1,),
        in_specs=[pl.BlockSpec((TOK_TILE, POOL_WIDTH), lambda i: (last, 0)),
                  pl.BlockSpec((TOK_TILE, 2 * CONV_WIDTH), lambda i: (last, 0)),
                  full((POOL_BUF, DEC_BATCH, POOL_WIDTH)), full((CONV_BUF, DEC_BATCH, CONV_WIDTH)),
                  full((POOL_WIDTH, POOL_WIDTH)), full((1, POOL_WIDTH)),
                  full((CONV_KERNEL, CONV_WIDTH)), full((1, CONV_WIDTH)), full((1, CONV_WIDTH)),
                  full((1, CONV_WIDTH))],
        out_specs=[full((TOK_TILE, POOL_WIDTH)), full((TOK_TILE, CONV_WIDTH)), full((DEC_BATCH, CONV_WIDTH))],
        out_shape=[jax.ShapeDtypeStruct((TOK_TILE, POOL_WIDTH), F32),
                   jax.ShapeDtypeStruct((TOK_TILE, CONV_WIDTH), F32),
                   jax.ShapeDtypeStruct((DEC_BATCH, CONV_WIDTH), F32)],
        compiler_params=_params("arbitrary"),
        name="mix_decode",
    )(u_pool, c_in, sp_t, sc_t, w_bd, scale, conv_w, conv_b, ln_g, ln_b)


def _attn_decode_tables():
    slopes = _alibi_slopes(ATTN_HEADS)
    pos = np.arange(CACHE_LEN)
    dist = (CACHE_LEN - pos).astype(np.float32)
    bias = np.zeros((8, CACHE_LEN), np.float32)
    for h in range(ATTN_HEADS):
        bias[h] = -slopes[h] * dist
    mult = np.zeros((1, CACHE_LEN), np.float32)
    for w, d in DILATED:
        mult[0] += ((CACHE_LEN - pos) % d == 0) & (CACHE_LEN - pos <= w)
    return bias, mult


def _round_bf16(x):
    return x.astype(BF16).astype(F32)


def _attn_decode_kernel(qkv_ref, k_ref, v_ref, bias_ref, mult_ref, y_ref):
    scale = HEAD_DIM ** -0.5
    mult = mult_ref[...]
    n_branch = float(len(DILATED))
    for h in range(ATTN_HEADS):
        q = _round_bf16(qkv_ref[0, h])
        k_new = _round_bf16(qkv_ref[1, h])
        v_new = _round_bf16(qkv_ref[2, h])
        s = jnp.sum(_round_bf16(k_ref[h]) * q, axis=0, keepdims=True) * scale + bias_ref[h:h + 1, :]
        s = jnp.where(mult > 0.0, s, NEG_BIG)
        s_self = jnp.sum(q * k_new, axis=0, keepdims=True) * scale
        m = jnp.maximum(jnp.max(s, axis=1, keepdims=True), s_self)
        p = mult * jnp.exp(s - m)
        p_self = n_branch * jnp.exp(s_self - m)
        l = jnp.sum(p, axis=1, keepdims=True) + p_self
        acc = jnp.sum(_round_bf16(v_ref[h]) * _round_bf16(p), axis=1, keepdims=True) + _round_bf16(p_self) * v_new
        y_ref[h] = acc / l


def _attn_decode(qkv_cols, cache_k_t, cache_v_t, bias, mult, layer):
    cache = pl.BlockSpec((None, None, ATTN_HEADS, HEAD_DIM, CACHE_LEN), lambda b: (layer, b, 0, 0, 0))
    return pl.pallas_call(
        _attn_decode_kernel,
        grid=(DEC_BATCH,),
        in_specs=[pl.BlockSpec((None, 3, ATTN_HEADS, HEAD_DIM, 1), lambda b: (b, 0, 0, 0, 0)),
                  cache, cache,
                  pl.BlockSpec((8, CACHE_LEN), lambda b: (0, 0)),
                  pl.BlockSpec((1, CACHE_LEN), lambda b: (0, 0))],
        out_specs=pl.BlockSpec((None, ATTN_HEADS, HEAD_DIM, 1), lambda b: (b, 0, 0, 0)),
        out_shape=jax.ShapeDtypeStruct((DEC_BATCH, ATTN_HEADS, HEAD_DIM, 1), F32),
        compiler_params=_params("arbitrary"),
        name="attn_decode",
    )(qkv_cols, cache_k_t, cache_v_t, bias, mult)


ROW_CHUNKS = D_MODEL // LANES


def _store_row_tiles(ref, rows):
    n = rows.shape[0]
    for c in range(ROW_CHUNKS):
        ref[pl.ds(c, n, stride=ROW_CHUNKS), :] = rows[:, c * LANES:(c + 1) * LANES]


def _load_row_tiles(ref, n):
    return jnp.concatenate([ref[pl.ds(c, n, stride=ROW_CHUNKS), :] for c in range(ROW_CHUNKS)], axis=1)


def _post_mix_kernel(x_ref, ya_ref, yb_ref, yc_ref, yad_ref, ybd_ref, ycd_ref, wo_ref, g_ref, b_ref, rw_ref, rb_ref,
                     tri_ref, x1_ref, x1t_ref, info_ref, gate_ref, cnt_ref, wob_ref, run_ref):
    i = pl.program_id(0)

    @pl.when(i == 0)
    def _():
        wob_ref[...] = wo_ref[...].astype(BF16)
        run_ref[...] = jnp.zeros_like(run_ref)

    is_decode = i == N_TOK_TILES - 1
    mixed = jnp.concatenate([jnp.where(is_decode, d_ref[...], p_ref[...]) for p_ref, d_ref in
                             ((ya_ref, yad_ref), (yb_ref, ybd_ref), (yc_ref, ycd_ref))], axis=1)
    mix = jnp.dot(mixed.astype(BF16), wob_ref[...], preferred_element_type=F32)
    x1 = _layer_norm(DEEPNORM_ALPHA * x_ref[...] + mix, g_ref[...], b_ref[...])
    x1_ref[...] = x1
    _store_row_tiles(x1t_ref, x1)
    logits = _bdot(x1, rw_ref[...]) + rb_ref[...]
    lane = lax.broadcasted_iota(jnp.int32, (TOK_TILE, LANES), 1).astype(F32)
    work = logits
    chosen = jnp.zeros((TOK_TILE, LANES), F32)
    tops, idxs, hots = [], [], []
    for _ in range(TOP_K):
        mk = jnp.max(work, axis=-1, keepdims=True)
        idx = jnp.min(jnp.where(work == mk, lane, float(LANES)), axis=-1, keepdims=True)
        hot = lane == idx
        tops.append(mk)
        idxs.append(idx)
        hots.append(hot)
        chosen = chosen + jnp.where(hot, 1.0, 0.0)
        work = jnp.where(hot, -jnp.inf, work)
    exps = [jnp.exp(v - tops[0]) for v in tops]
    denom = exps[0] + exps[1] + exps[2] + exps[3]
    ahead = jnp.dot(tri_ref[...], chosen.astype(BF16), preferred_element_type=F32) + run_ref[0:1, :]
    ranks = [jnp.sum(jnp.where(hot, ahead, 0.0), axis=-1, keepdims=True) for hot in hots]
    info = jnp.zeros((TOK_TILE, LANES), F32)
    gates = jnp.zeros((TOK_TILE, LANES), F32)
    for k in range(TOP_K):
        info = jnp.where(lane == float(k), idxs[k], info)
        info = jnp.where(lane == float(TOP_K + k), ranks[k], info)
        gates = jnp.where(lane == float(k), exps[k] / denom, gates)
    info_ref[...] = info.astype(jnp.int32)
    gate_ref[...] = gates
    run_ref[...] = run_ref[...] + jnp.sum(chosen, axis=0, keepdims=True)
    cnt_ref[...] = run_ref[...]


def _post_mix(x, mix_prompt, mix_decode, w_o, ln_g, ln_b, router_w, router_b, tri, layer):
    row = pl.BlockSpec((TOK_TILE, D_MODEL), lambda i: (i, 0))
    vec = pl.BlockSpec((1, D_MODEL), lambda i: (0, 0))
    lanes = pl.BlockSpec((TOK_TILE, LANES), lambda i: (i, 0))
    widths = (POOL_WIDTH, ATTN_WIDTH, CONV_WIDTH)
    prompt = [pl.BlockSpec((TOK_TILE, w), lambda i: (jnp.minimum(i, N_PROMPT_TILES - 1), 0)) for w in widths]
    decode = [pl.BlockSpec((TOK_TILE, w), lambda i: (0, 0)) for w in widths]
    return pl.pallas_call(
        _post_mix_kernel,
        grid=(N_TOK_TILES,),
        in_specs=[row] + prompt + decode + [
                  pl.BlockSpec((None, D_MODEL, D_MODEL), lambda i: (layer, 0, 0), pipeline_mode=pl.Buffered(1)),
                  vec, vec,
                  pl.BlockSpec((D_MODEL, LANES), lambda i: (0, 0)),
                  pl.BlockSpec((1, LANES), lambda i: (0, 0)),
                  pl.BlockSpec((TOK_TILE, TOK_TILE), lambda i: (0, 0))],
        out_specs=[row, pl.BlockSpec((TOK_TILE * ROW_CHUNKS, LANES), lambda i: (i, 0)), lanes, lanes,
                   pl.BlockSpec((8, LANES), lambda i: (0, 0))],
        out_shape=[jax.ShapeDtypeStruct((M_TOT, D_MODEL), F32),
                   jax.ShapeDtypeStruct((M_TOT * ROW_CHUNKS, LANES), F32),
                   jax.ShapeDtypeStruct((M_TOT, LANES), jnp.int32),
                   jax.ShapeDtypeStruct((M_TOT, LANES), F32),
                   jax.ShapeDtypeStruct((8, LANES), F32)],
        scratch_shapes=[pltpu.VMEM((D_MODEL, D_MODEL), BF16), pltpu.VMEM((8, LANES), F32)],
        compiler_params=_params("arbitrary"),
        name="post_mix",
    )(x, *mix_prompt, *mix_decode, w_o, ln_g, ln_b, router_w, router_b, tri)


def _dispatch_kernel(dest_ref, fill_ref, nu_ref, x_ref, xs_hbm, zero_ref, sem):
    i = pl.program_id(0)

    @pl.when(i == 0)
    def _():
        zero_ref[...] = jnp.zeros_like(zero_ref)

        def tile_copy(start):
            return pltpu.make_async_copy(zero_ref, xs_hbm.at[pl.ds(start, EXPERT_TILE)], sem)

        def over_partial_tiles(act):
            def body(e, carry):
                @pl.when(fill_ref[e] >= 0)
                def _():
                    act(tile_copy(fill_ref[e]))
                return carry
            lax.fori_loop(0, N_EXPERTS, body, 0)

        def over_unused_tiles(act):
            def body(tile, carry):
                act(tile_copy(tile * EXPERT_TILE))
                return carry
            lax.fori_loop(nu_ref[0], N_EXPERT_TILES, body, 0)

        over_partial_tiles(lambda c: c.start())
        over_unused_tiles(lambda c: c.start())
        over_partial_tiles(lambda c: c.wait())
        over_unused_tiles(lambda c: c.wait())

    base = i * (TOK_TILE * TOP_K)
    for r in range(TOK_TILE):
        for k in range(TOP_K):
            pltpu.make_async_copy(x_ref.at[r], xs_hbm.at[dest_ref[base + r * TOP_K + k]], sem).start(priority=k % 2)
    for _ in range(TOP_K):
        pltpu.make_async_copy(x_ref, xs_hbm.at[pl.ds(0, TOK_TILE)], sem).wait()


def _dispatch(dest_flat, fill_start, n_used, x1_tiles):
    return pl.pallas_call(
        _dispatch_kernel,
        grid_spec=pltpu.PrefetchScalarGridSpec(
            num_scalar_prefetch=3,
            grid=(N_TOK_TILES,),
            in_specs=[pl.BlockSpec((TOK_TILE, ROW_CHUNKS, LANES), lambda i, d, f, n: (i, 0, 0))],
            out_specs=pl.BlockSpec(memory_space=pl.ANY),
            scratch_shapes=[pltpu.VMEM((EXPERT_TILE, ROW_CHUNKS, LANES), F32), pltpu.SemaphoreType.DMA]),
        out_shape=jax.ShapeDtypeStruct((N_SORTED, ROW_CHUNKS, LANES), F32),
        compiler_params=_params("arbitrary"),
        name="moe_dispatch",
    )(dest_flat, fill_start, n_used, x1_tiles)


def _experts_kernel(te_ref, nu_ref, x_ref, w1_ref, b1_ref, w2_ref, b2_ref, o_ref, w1b_ref, w2b_ref):
    i = pl.program_id(0)
    e = te_ref[i]
    e_prev = te_ref[jnp.maximum(i - 1, 0)]

    @pl.when((i == 0) | (e != e_prev))
    def _():
        w1b_ref[...] = w1_ref[...].astype(BF16)
        w2b_ref[...] = w2_ref[...].astype(BF16)

    @pl.when(i < nu_ref[0])
    def _():
        x = _load_row_tiles(x_ref, EXPERT_TILE)
        h = jnp.dot(x.astype(BF16), w1b_ref[...], preferred_element_type=F32) + b1_ref[...]
        gate = jnp.minimum(h[:, :D_EXPERT], SWIGLU_LIMIT)
        up = jnp.clip(h[:, D_EXPERT:], -SWIGLU_LIMIT, SWIGLU_LIMIT)
        act = gate * jax.nn.sigmoid(SWIGLU_ALPHA * gate) * (up + 1.0)
        _store_row_tiles(o_ref, jnp.dot(act.astype(BF16), w2b_ref[...], preferred_element_type=F32) + b2_ref[...])

    @pl.when(i >= nu_ref[0])
    def _():
        o_ref[...] = jnp.zeros_like(o_ref)


def _experts(tile_expert, n_used, xs, w1, b1, w2, b2, layer):
    by_expert = lambda *shape: pl.BlockSpec((None, None) + shape, lambda i, te, nu: (layer, te[i], 0, 0))
    rows = pl.BlockSpec((EXPERT_TILE * ROW_CHUNKS, LANES), lambda i, te, nu: (jnp.minimum(i, nu[0] - 1), 0))
    return pl.pallas_call(
        _experts_kernel,
        grid_spec=pltpu.PrefetchScalarGridSpec(
            num_scalar_prefetch=2,
            grid=(N_EXPERT_TILES,),
            in_specs=[rows, by_expert(D_MODEL, 2 * D_EXPERT), by_expert(1, 2 * D_EXPERT),
                      by_expert(D_EXPERT, D_MODEL), by_expert(1, D_MODEL)],
            out_specs=pl.BlockSpec((EXPERT_TILE * ROW_CHUNKS, LANES), lambda i, te, nu: (i, 0)),
            scratch_shapes=[pltpu.VMEM((D_MODEL, 2 * D_EXPERT), BF16), pltpu.VMEM((D_EXPERT, D_MODEL), BF16)]),
        out_shape=jax.ShapeDtypeStruct((N_SORTED * ROW_CHUNKS, LANES), F32),
        compiler_params=_params("arbitrary"),
        name="moe_experts",
    )(tile_expert, n_used, xs, w1, b1, w2, b2)


def _combine_kernel(dest_ref, x1_ref, gate_ref, g_ref, b_ref, ys_hbm, o_ref, rows_ref, sem):
    i = pl.program_id(0)

    def gather(tile, slot):
        base = tile * (TOK_TILE * TOP_K)
        for r in range(TOK_TILE):
            for k in range(TOP_K):
                pltpu.make_async_copy(ys_hbm.at[dest_ref[base + r * TOP_K + k]],
                                      rows_ref.at[slot, k, pl.ds(r * ROW_CHUNKS, ROW_CHUNKS), :],
                                      sem.at[slot]).start(priority=k % 2)

    @pl.when(i == 0)
    def _():
        gather(0, 0)

    for parity in range(2):
        @pl.when((i + 1 < N_TOK_TILES) & (i % 2 == parity))
        def _():
            gather(i + 1, 1 - parity)

    slot = i % 2
    for _ in range(TOP_K):
        pltpu.make_async_copy(ys_hbm.at[pl.ds(0, TOK_TILE)], ys_hbm.at[pl.ds(0, TOK_TILE)], sem.at[slot]).wait()
    gates = gate_ref[...]
    moe = gates[:, 0:1] * _load_row_tiles(rows_ref.at[slot, 0], TOK_TILE)
    for k in range(1, TOP_K):
        moe = moe + gates[:, k:k + 1] * _load_row_tiles(rows_ref.at[slot, k], TOK_TILE)
    o_ref[...] = _layer_norm(DEEPNORM_ALPHA * x1_ref[...] + moe, g_ref[...], b_ref[...])


def _combine(dest_flat, x1, gates, ln_g, ln_b, ys_tiles):
    row = pl.BlockSpec((TOK_TILE, D_MODEL), lambda i, d: (i, 0))
    vec = pl.BlockSpec((1, D_MODEL), lambda i, d: (0, 0))
    return pl.pallas_call(
        _combine_kernel,
        grid_spec=pltpu.PrefetchScalarGridSpec(
            num_scalar_prefetch=1,
            grid=(N_TOK_TILES,),
            in_specs=[row, pl.BlockSpec((TOK_TILE, LANES), lambda i, d: (i, 0)), vec, vec,
                      pl.BlockSpec(memory_space=pl.ANY)],
            out_specs=row,
            scratch_shapes=[pltpu.VMEM((2, TOP_K, TOK_TILE * ROW_CHUNKS, LANES), F32),
                            pltpu.SemaphoreType.DMA((2,))]),
        out_shape=jax.ShapeDtypeStruct((M_TOT, D_MODEL), F32),
        compiler_params=_params("arbitrary"),
        name="moe_combine",
    )(dest_flat, x1, gates, ln_g, ln_b, ys_tiles)


def _routing_tables(info, counts):
    top_i = info[:, :TOP_K]
    rank = info[:, TOP_K:2 * TOP_K]
    cnt = counts[0, :N_EXPERTS].astype(jnp.int32)
    padded = (cnt + EXPERT_TILE - 1) // EXPERT_TILE * EXPERT_TILE
    pad_end = jnp.cumsum(padded)
    pad_start = pad_end - padded
    hot = top_i[:, :, None] == jnp.arange(N_EXPERTS, dtype=jnp.int32)
    dest = rank + jnp.sum(jnp.where(hot, pad_start, 0), axis=-1)
    fill_start = jnp.where(cnt % EXPERT_TILE != 0, pad_end - EXPERT_TILE, -1).astype(jnp.int32)
    tile_start = jnp.arange(N_EXPERT_TILES, dtype=jnp.int32) * EXPERT_TILE
    n_used = (pad_end[-1] // EXPERT_TILE).astype(jnp.int32)
    tile_expert = jnp.sum(tile_start[:, None] >= pad_end[None, :], axis=1).astype(jnp.int32)
    last_expert = tile_expert[jnp.maximum(n_used - 1, 0)]
    tile_expert = jnp.where(tile_start < pad_end[-1], tile_expert, last_expert)
    return dest.reshape(-1).astype(jnp.int32), fill_start, tile_expert, n_used.reshape(1)


def _block_diag(pool_w):
    g, n, _ = pool_w.shape
    out = jnp.zeros((g * n, g * n), pool_w.dtype)
    for i in range(g):
        out = out.at[i * n:(i + 1) * n, i * n:(i + 1) * n].set(pool_w[i])
    return out


def kernel(x_prompt, x_sample, state_pool, state_conv, cache_attn_k, cache_attn_v, w_in, pool_w, pool_scale, conv_w, conv_b, conv_ln_g, conv_ln_b, w_o, ln1_g, ln1_b, router_w, router_b, expert_w_in, expert_b_in, expert_w_out, expert_b_out, ln2_g, ln2_b):
    bias_prompt = jnp.asarray(_attn_bias_tables())
    bias_decode, mult_decode = (jnp.asarray(t) for t in _attn_decode_tables())
    tri = jnp.asarray(np.tril(np.ones((TOK_TILE, TOK_TILE), np.float32), -1), BF16)
    x = jnp.concatenate([x_prompt.reshape(M_PROMPT, D_MODEL), x_sample.reshape(DEC_BATCH, D_MODEL),
                         jnp.zeros((M_TOT - M_PROMPT - DEC_BATCH, D_MODEL), F32)], axis=0)
    cache_k_t = cache_attn_k.transpose(0, 1, 3, 4, 2)
    cache_v_t = cache_attn_v.transpose(0, 1, 3, 4, 2)
    b_in = expert_b_in.reshape(DEPTH, N_EXPERTS, 1, 2 * D_EXPERT)
    b_out = expert_b_out.reshape(DEPTH, N_EXPERTS, 1, D_MODEL)
    outs = {name: [] for name in ("p_pool", "p_conv", "p_k", "p_v", "s_pool", "s_conv", "s_k", "s_v")}
    for l in range(DEPTH):
        u_pool, c_in, qkv_d, q1, q4, q16, kv_t = _in_proj(x, w_in, l)
        w_bd = _block_diag(pool_w[l])
        scale = pool_scale[l].reshape(1, POOL_WIDTH)
        cb, cg, cbb = (v[l].reshape(1, CONV_WIDTH) for v in (conv_b, conv_ln_g, conv_ln_b))

        ya = _pool_prompt(u_pool, w_bd, scale)
        yc, glu = _conv_prompt(c_in, conv_w[l], cb, cg, cbb)
        branches = [_attn_prompt(q.reshape(-1, QKV_WIDTH), bias_prompt[bi], d)
                    for bi, (q, (_, d)) in enumerate(zip((q1, q4, q16), DILATED))]
        yb = _attn_merge(branches)

        sp_t = state_pool[l].transpose(1, 0, 2)
        sc_t = state_conv[l].transpose(1, 0, 2)
        ya_d, yc_d, glu_d = _mix_decode(u_pool, c_in, sp_t, sc_t, w_bd, scale, conv_w[l], cb, cg, cbb)
        qkv_d = qkv_d[:DEC_BATCH]
        yb_d = _attn_decode(qkv_d.reshape(DEC_BATCH, 3, ATTN_HEADS, HEAD_DIM, 1), cache_k_t, cache_v_t,
                            bias_decode, mult_decode, l)
        yb_d = jnp.pad(yb_d.reshape(DEC_BATCH, ATTN_WIDTH), ((0, TOK_TILE - DEC_BATCH), (0, 0)))

        rw = jnp.pad(router_w[l], ((0, 0), (0, LANES - N_EXPERTS)))
        rb = jnp.pad(router_b[l], (0, LANES - N_EXPERTS), constant_values=NEG_BIG).reshape(1, LANES)
        x1, x1_tiles, info, gates, counts = _post_mix(x, (ya, yb, yc), (ya_d, yb_d, yc_d), w_o,
                                                      ln1_g[l].reshape(1, D_MODEL), ln1_b[l].reshape(1, D_MODEL),
                                                      rw, rb, tri, l)

        dest, fill_start, tile_expert, n_used = _routing_tables(info, counts)
        xs = _dispatch(dest, fill_start, n_used, x1_tiles.reshape(M_TOT, ROW_CHUNKS, LANES))
        ys = _experts(tile_expert, n_used, xs.reshape(N_SORTED * ROW_CHUNKS, LANES), expert_w_in, b_in,
                      expert_w_out, b_out, l)
        x = _combine(dest, x1, gates, ln2_g[l].reshape(1, D_MODEL), ln2_b[l].reshape(1, D_MODEL),
                     ys.reshape(N_SORTED, ROW_CHUNKS, LANES))

        kv_t = kv_t[:BATCH].reshape(BATCH, 2, ATTN_HEADS, HEAD_DIM, CACHE_LEN).transpose(1, 0, 4, 2, 3)
        last_rows = lambda a, n: a[:M_PROMPT].reshape(BATCH, SEQ, a.shape[-1])[:, SEQ - n:]
        outs["p_pool"].append(last_rows(u_pool, POOL_BUF))
        outs["p_conv"].append(last_rows(glu, CONV_BUF))
        outs["p_k"].append(kv_t[0])
        outs["p_v"].append(kv_t[1])
        u_d = u_pool[M_PROMPT:M_PROMPT + DEC_BATCH]
        outs["s_pool"].append(jnp.concatenate([state_pool[l][:, 1:], u_d[:, None]], axis=1))
        outs["s_conv"].append(jnp.concatenate([state_conv[l][:, 1:], glu_d[:, None]], axis=1))
        outs["s_k"].append(qkv_d[:, ATTN_WIDTH:2 * ATTN_WIDTH].reshape(DEC_BATCH, 1, ATTN_HEADS, HEAD_DIM))
        outs["s_v"].append(qkv_d[:, 2 * ATTN_WIDTH:].reshape(DEC_BATCH, 1, ATTN_HEADS, HEAD_DIM))

    y_p = x[:M_PROMPT].reshape(BATCH, SEQ, D_MODEL)
    y_s = x[M_PROMPT:M_PROMPT + DEC_BATCH].reshape(DEC_BATCH, 1, D_MODEL)
    return (y_p, y_s) + tuple(jnp.stack(outs[name]) for name in
                              ("p_pool", "p_conv", "p_k", "p_v", "s_pool", "s_conv", "s_k", "s_v"))
```

```python
import functools
import math

import numpy as np
import jax
import jax.numpy as jnp
from jax import lax
from jax.experimental import pallas as pl
from jax.experimental.pallas import tpu as pltpu

F32 = jnp.float32
BF16 = jnp.bfloat16

D_MODEL = 1024
BATCH = 4
SEQ = 4096
DEPTH = 4
DEC_BATCH = 32
PAST_LEN = 8192
POOL_WIDTH = 256
POOL_WINDOWS = (2, 4, 8, 16)
POOL_BUF = 15
ATTN_HEADS = 6
HEAD_DIM = 64
ATTN_WIDTH = ATTN_HEADS * HEAD_DIM
DILATED = ((128, 1), (512, 4), (2048, 16))
ATTN_BLOCK = 128
CACHE_LEN = 2048
CONV_WIDTH = 384
CONV_KERNEL = 31
CONV_BUF = 30
N_EXPERTS = 32
TOP_K = 4
D_EXPERT = 1024
SWIGLU_ALPHA = 1.702
SWIGLU_LIMIT = 7.0
DEEPNORM_ALPHA = (2 * DEPTH) ** 0.25
LN_EPS = 1e-5
NEG_BIG = -1e30
LANES = 128

M_PROMPT = BATCH * SEQ
TOK_TILE = 256
M_TOT = M_PROMPT + TOK_TILE
N_TOK_TILES = M_TOT // TOK_TILE
SEQ_TILE = 512
ATTN_Q_TILE = 512
EXPERT_TILE = 512
N_ASSIGN = M_TOT * TOP_K
N_EXPERT_TILES = -(-N_ASSIGN // EXPERT_TILE) + N_EXPERTS
N_SORTED = N_EXPERT_TILES * EXPERT_TILE
VMEM_LIMIT = 56 * 1024 * 1024


def _alibi_slopes(n):
    def pow2(k):
        start = 2.0 ** (-8.0 / k)
        return [start ** (i + 1) for i in range(k)]
    if math.log2(n).is_integer():
        return pow2(n)
    c = 2 ** math.floor(math.log2(n))
    return pow2(c) + pow2(2 * c)[0::2][: n - c]


def _bdot(a, b):
    return jnp.dot(a.astype(BF16), b.astype(BF16), preferred_element_type=F32)


def _layer_norm(x, g, b):
    mu = jnp.mean(x, axis=-1, keepdims=True)
    xc = x - mu
    var = jnp.mean(xc * xc, axis=-1, keepdims=True)
    return xc * lax.rsqrt(var + LN_EPS) * g + b


def _params(*sem):
    return pltpu.CompilerParams(dimension_semantics=sem, vmem_limit_bytes=VMEM_LIMIT)


N_PROMPT_TILES = M_PROMPT // TOK_TILE
TILES_PER_SEQ = SEQ // TOK_TILE
QKV_WIDTH = 3 * ATTN_WIDTH
QKV_CHUNKS = QKV_WIDTH // LANES


def _in_proj_kernel(x_ref, w_ref, pool_ref, conv_ref, qkvd_ref, q1_ref, q4_ref, q16_ref, kvt_ref, wb_ref, stage_ref):
    @pl.when(pl.program_id(0) == 0)
    def _():
        wb_ref[...] = w_ref[...].astype(BF16)

    res = jnp.dot(x_ref[...].astype(BF16), wb_ref[...], preferred_element_type=F32)
    qkv = res[:, POOL_WIDTH:POOL_WIDTH + QKV_WIDTH]
    pool_ref[...] = res[:, :POOL_WIDTH]
    conv_ref[...] = res[:, POOL_WIDTH + QKV_WIDTH:]
    qkvd_ref[...] = qkv
    kvt_ref[...] = jnp.transpose(qkv[:, ATTN_WIDTH:])
    q1_ref[...] = qkv.astype(BF16)
    for c in range(QKV_CHUNKS):
        stage_ref[c] = qkv[:, c * LANES:(c + 1) * LANES]
    for ref, d in ((q4_ref, DILATED[1][1]), (q16_ref, DILATED[2][1])):
        n = TOK_TILE // d
        for r in range(d):
            rows = [stage_ref[c, pl.ds(r, n, stride=d), :] for c in range(QKV_CHUNKS)]
            ref[r] = jnp.concatenate(rows, axis=1).astype(BF16)


CACHE_TILE0 = (SEQ - CACHE_LEN) // TOK_TILE


def _in_proj(x, w_in, layer):
    n_in = w_in.shape[2]
    d4, d16 = DILATED[1][1], DILATED[2][1]
    tile = lambda s: (s + N_TOK_TILES - 1) % N_TOK_TILES
    ptile = lambda s: jnp.maximum(s - 1, 0)
    by_residue = lambda d: pl.BlockSpec(
        (None, d, TOK_TILE // d, QKV_WIDTH), lambda s: (ptile(s) // TILES_PER_SEQ, 0, ptile(s) % TILES_PER_SEQ, 0))
    return pl.pallas_call(
        _in_proj_kernel,
        grid=(N_TOK_TILES,),
        in_specs=[pl.BlockSpec((TOK_TILE, D_MODEL), lambda s: (tile(s), 0)),
                  pl.BlockSpec((None, D_MODEL, n_in), lambda s: (layer, 0, 0), pipeline_mode=pl.Buffered(1))],
        out_specs=[pl.BlockSpec((TOK_TILE, POOL_WIDTH), lambda s: (tile(s), 0)),
                   pl.BlockSpec((TOK_TILE, 2 * CONV_WIDTH), lambda s: (tile(s), 0)),
                   pl.BlockSpec((TOK_TILE, QKV_WIDTH), lambda s: (jnp.minimum(s, 1), 0)),
                   pl.BlockSpec((TOK_TILE, QKV_WIDTH), lambda s: (tile(s), 0)),
                   by_residue(d4), by_residue(d16),
                   pl.BlockSpec((None, 2 * ATTN_WIDTH, TOK_TILE),
                                lambda s: (ptile(s) // TILES_PER_SEQ, 0,
                                           jnp.maximum(ptile(s) % TILES_PER_SEQ - CACHE_TILE0, 0)))],
        out_shape=[jax.ShapeDtypeStruct((M_TOT, POOL_WIDTH), F32),
                   jax.ShapeDtypeStruct((M_TOT, 2 * CONV_WIDTH), F32),
                   jax.ShapeDtypeStruct((2 * TOK_TILE, QKV_WIDTH), F32),
                   jax.ShapeDtypeStruct((M_TOT, QKV_WIDTH), BF16),
                   jax.ShapeDtypeStruct((BATCH, d4, SEQ // d4, QKV_WIDTH), BF16),
                   jax.ShapeDtypeStruct((BATCH, d16, SEQ // d16, QKV_WIDTH), BF16),
                   jax.ShapeDtypeStruct((BATCH, 2 * ATTN_WIDTH, CACHE_LEN), F32)],
        scratch_shapes=[pltpu.VMEM((D_MODEL, n_in), BF16), pltpu.VMEM((QKV_CHUNKS, TOK_TILE, LANES), F32)],
        compiler_params=_params("arbitrary"),
        name="in_proj",
    )(x, w_in)


def _pool_select(lane, vals):
    out = vals[-1]
    for g in range(len(vals) - 2, -1, -1):
        out = jnp.where(lane < (g + 1) * 64, vals[g], out)
    return out


def _pool_prompt_kernel(u_ref, w_ref, scale_ref, o_ref, halo_ref):
    t = pl.program_id(1)

    @pl.when(t == 0)
    def _():
        halo_ref[...] = jnp.zeros_like(halo_ref)

    u = u_ref[...]
    n = SEQ_TILE + 16
    ext = jnp.concatenate([halo_ref[...], u], axis=0)
    p2 = ext + pltpu.roll(ext, 1, 0)
    p4 = p2 + pltpu.roll(p2, 2, 0)
    p8 = p4 + pltpu.roll(p4, 4, 0)
    p16 = p8 + pltpu.roll(p8, 8, 0)
    lane = lax.broadcasted_iota(jnp.int32, (SEQ_TILE, POOL_WIDTH), 1)
    row = lax.broadcasted_iota(jnp.int32, (SEQ_TILE, POOL_WIDTH), 0)
    pos1 = (row + t * SEQ_TILE + 1).astype(F32)
    wsel = _pool_select(lane, [jnp.full((SEQ_TILE, POOL_WIDTH), float(w), F32) for w in POOL_WINDOWS])
    cnt = jnp.minimum(wsel, pos1)
    sums = _pool_select(lane, [p[16:n] for p in (p2, p4, p8, p16)])
    d = sums / cnt - u
    o_ref[...] = _bdot(d, w_ref[...]) * scale_ref[...]
    halo_ref[...] = u[SEQ_TILE - 16:]


def _pool_prompt(u_pool, w_bd, scale):
    nt = SEQ // SEQ_TILE
    return pl.pallas_call(
        _pool_prompt_kernel,
        grid=(BATCH, nt),
        in_specs=[pl.BlockSpec((SEQ_TILE, POOL_WIDTH), lambda b, t: (b * nt + t, 0)),
                  pl.BlockSpec((POOL_WIDTH, POOL_WIDTH), lambda b, t: (0, 0)),
                  pl.BlockSpec((1, POOL_WIDTH), lambda b, t: (0, 0))],
        out_specs=pl.BlockSpec((SEQ_TILE, POOL_WIDTH), lambda b, t: (b * nt + t, 0)),
        out_shape=jax.ShapeDtypeStruct((M_PROMPT, POOL_WIDTH), F32),
        scratch_shapes=[pltpu.VMEM((16, POOL_WIDTH), F32)],
        compiler_params=_params("arbitrary", "arbitrary"),
        name="pool_prompt",
    )(u_pool, w_bd, scale)


CONV_HALO = 32


def _conv_prompt_kernel(c_ref, w_ref, b_ref, g_ref, bb_ref, y_ref, u_ref, ext_ref, sh_ref):
    t = pl.program_id(1)

    @pl.when(t == 0)
    def _():
        ext_ref[0:CONV_HALO, :] = jnp.zeros((CONV_HALO, CONV_WIDTH), F32)

    c = c_ref[...]
    u = c[:, :CONV_WIDTH] * jax.nn.sigmoid(c[:, CONV_WIDTH:])
    u_ref[...] = u
    ext_ref[CONV_HALO:CONV_HALO + SEQ_TILE, :] = u
    span = SEQ_TILE + CONV_HALO - 8
    for ph in range(1, 8):
        sh_ref[ph, 0:span, :] = ext_ref[pl.ds(ph, span), :]

    acc = jnp.zeros((SEQ_TILE, CONV_WIDTH), F32)
    for j in range(CONV_KERNEL):
        off = j + 2
        a, ph = off // 8, off % 8
        tap = ext_ref[8 * a:8 * a + SEQ_TILE, :] if ph == 0 else sh_ref[ph, 8 * a:8 * a + SEQ_TILE, :]
        acc = acc + tap * w_ref[j:j + 1, :]
    y = _layer_norm(acc + b_ref[...], g_ref[...], bb_ref[...])
    y_ref[...] = y * jax.nn.sigmoid(y)
    ext_ref[0:CONV_HALO, :] = u[SEQ_TILE - CONV_HALO:]


def _conv_prompt(c_in, conv_w, conv_b, ln_g, ln_b):
    nt = SEQ // SEQ_TILE
    vec = pl.BlockSpec((1, CONV_WIDTH), lambda b, t: (0, 0))
    return pl.pallas_call(
        _conv_prompt_kernel,
        grid=(BATCH, nt),
        in_specs=[pl.BlockSpec((SEQ_TILE, 2 * CONV_WIDTH), lambda b, t: (b * nt + t, 0)),
                  pl.BlockSpec((CONV_KERNEL, CONV_WIDTH), lambda b, t: (0, 0)),
                  vec, vec, vec],
        out_specs=[pl.BlockSpec((SEQ_TILE, CONV_WIDTH), lambda b, t: (b * nt + t, 0)),
                   pl.BlockSpec((SEQ_TILE, CONV_WIDTH), lambda b, t: (b * nt + t, 0))],
        out_shape=[jax.ShapeDtypeStruct((M_PROMPT, CONV_WIDTH), F32),
                   jax.ShapeDtypeStruct((M_PROMPT, CONV_WIDTH), F32)],
        scratch_shapes=[pltpu.VMEM((SEQ_TILE + CONV_HALO, CONV_WIDTH), F32),
                        pltpu.VMEM((8, SEQ_TILE + CONV_HALO, CONV_WIDTH), F32)],
        compiler_params=_params("arbitrary", "arbitrary"),
        name="conv_prompt",
    )(c_in, conv_w, conv_b, ln_g, ln_b)


def _attn_bias_tables():
    slopes = _alibi_slopes(ATTN_HEADS)
    qi = np.arange(ATTN_BLOCK)[:, None] + ATTN_BLOCK
    ki = np.arange(2 * ATTN_BLOCK)[None, :]
    delta = qi - ki
    tabs = np.zeros((len(DILATED), ATTN_HEADS // 2, 4, ATTN_BLOCK, 2 * ATTN_BLOCK), np.float32)
    for bi, (w, d) in enumerate(DILATED):
        n_back = w // d
        band = (delta >= 0) & (delta <= n_back)
        for h in range(ATTN_HEADS):
            bias = -slopes[h] * (delta * d).astype(np.float32)
            for first in range(2):
                valid = band & (ki >= ATTN_BLOCK) if first else band
                tabs[bi, h // 2, 2 * (h % 2) + first] = np.where(valid, bias, NEG_BIG)
    return tabs


def _attn_prompt_kernel(blocks_per_seq, q_ref, kp_ref, kc_ref, vp_ref, vc_ref, bias_ref, o_ref, lse_ref):
    t = pl.program_id(1)
    kk = jnp.concatenate([kp_ref[...], kc_ref[...]], axis=0)
    vv = jnp.concatenate([vp_ref[...], vc_ref[...]], axis=0)
    lane_q = lax.broadcasted_iota(jnp.int32, (ATTN_BLOCK, 2 * HEAD_DIM), 1)
    for j in range(ATTN_Q_TILE // ATTN_BLOCK):
        blk = t * (ATTN_Q_TILE // ATTN_BLOCK) + j
        first = (blk % blocks_per_seq == 0).astype(jnp.int32)
        q = q_ref[j * ATTN_BLOCK:(j + 1) * ATTN_BLOCK, :]
        keys = kk[j * ATTN_BLOCK:(j + 2) * ATTN_BLOCK]
        vals = vv[j * ATTN_BLOCK:(j + 2) * ATTN_BLOCK]
        outs, lses = [], []
        for h in range(2):
            in_head = (lane_q >= h * HEAD_DIM) & (lane_q < (h + 1) * HEAD_DIM)
            qh = jnp.where(in_head, q, jnp.zeros_like(q))
            s = lax.dot_general(qh, keys, (((1,), (1,)), ((), ())), preferred_element_type=F32)
            s = s * (HEAD_DIM ** -0.5) + bias_ref[2 * h + first]
            m = jnp.max(s, axis=-1, keepdims=True)
            p = jnp.exp(s - m)
            l = jnp.sum(p, axis=-1, keepdims=True)
            acc = jnp.dot(p.astype(BF16), vals, preferred_element_type=F32)
            outs.append(acc / l)
            lses.append(m + jnp.log(l))
        o_ref[j * ATTN_BLOCK:(j + 1) * ATTN_BLOCK, :] = jnp.where(lane_q < HEAD_DIM, outs[0], outs[1])
        lse_ref[j * ATTN_BLOCK:(j + 1) * ATTN_BLOCK, :] = jnp.where(lane_q < HEAD_DIM, lses[0], lses[1])


def _attn_prompt(qkv, bias_tab, dilation):
    n_hp = ATTN_HEADS // 2
    per_tile = ATTN_Q_TILE // ATTN_BLOCK
    nt = M_PROMPT // ATTN_Q_TILE
    width = 2 * HEAD_DIM
    cur = lambda off: pl.BlockSpec((ATTN_Q_TILE, width), lambda hp, t: (t, off * n_hp + hp))
    prev = lambda off: pl.BlockSpec((ATTN_BLOCK, width),
                                    lambda hp, t: (jnp.maximum(t * per_tile - 1, 0), off * n_hp + hp))
    out = pl.BlockSpec((ATTN_Q_TILE, width), lambda hp, t: (t, hp))
    return pl.pallas_call(
        functools.partial(_attn_prompt_kernel, SEQ // dilation // ATTN_BLOCK),
        grid=(n_hp, nt),
        in_specs=[cur(0), prev(1), cur(1), prev(2), cur(2),
                  pl.BlockSpec((None, 4, ATTN_BLOCK, 2 * ATTN_BLOCK), lambda hp, t: (hp, 0, 0, 0))],
        out_specs=[out, out],
        out_shape=[jax.ShapeDtypeStruct((M_PROMPT, ATTN_WIDTH), F32),
                   jax.ShapeDtypeStruct((M_PROMPT, ATTN_WIDTH), F32)],
        compiler_params=_params("arbitrary", "arbitrary"),
        name=f"attn_prompt_d{dilation}",
    )(qkv, qkv, qkv, qkv, qkv, bias_tab)


ATTN_CHUNKS = ATTN_WIDTH // LANES


def _attn_merge_kernel(o1_ref, l1_ref, o4_ref, l4_ref, o16_ref, l16_ref, y_ref, so_ref, sl_ref):
    for j, (o_ref, l_ref, d) in enumerate(((o4_ref, l4_ref, DILATED[1][1]), (o16_ref, l16_ref, DILATED[2][1]))):
        n = TOK_TILE // d
        for r in range(d):
            ov, lv = o_ref[r], l_ref[r]
            for c in range(ATTN_CHUNKS):
                so_ref[j, c, pl.ds(r, n, stride=d), :] = ov[:, c * LANES:(c + 1) * LANES]
                sl_ref[j, c, pl.ds(r, n, stride=d), :] = lv[:, c * LANES:(c + 1) * LANES]
    outs = [o1_ref[...]] + [jnp.concatenate([so_ref[j, c] for c in range(ATTN_CHUNKS)], axis=1) for j in range(2)]
    lses = [l1_ref[...]] + [jnp.concatenate([sl_ref[j, c] for c in range(ATTN_CHUNKS)], axis=1) for j in range(2)]
    top = jnp.maximum(jnp.maximum(lses[0], lses[1]), lses[2])
    ws = [jnp.exp(v - top) for v in lses]
    y_ref[...] = (ws[0] * outs[0] + ws[1] * outs[1] + ws[2] * outs[2]) / (ws[0] + ws[1] + ws[2])


def _attn_merge(branches):
    (o1, l1), (o4, l4), (o16, l16) = branches
    d4, d16 = DILATED[1][1], DILATED[2][1]
    nat = pl.BlockSpec((TOK_TILE, ATTN_WIDTH), lambda i: (i, 0))
    by_residue = lambda d: pl.BlockSpec((None, d, TOK_TILE // d, ATTN_WIDTH),
                                        lambda i: (i // TILES_PER_SEQ, 0, i % TILES_PER_SEQ, 0))
    view = lambda a, d: a.reshape(BATCH, d, SEQ // d, ATTN_WIDTH)
    return pl.pallas_call(
        _attn_merge_kernel,
        grid=(N_PROMPT_TILES,),
        in_specs=[nat, nat, by_residue(d4), by_residue(d4), by_residue(d16), by_residue(d16)],
        out_specs=nat,
        out_shape=jax.ShapeDtypeStruct((M_PROMPT, ATTN_WIDTH), F32),
        scratch_shapes=[pltpu.VMEM((2, ATTN_CHUNKS, TOK_TILE, LANES), F32),
                        pltpu.VMEM((2, ATTN_CHUNKS, TOK_TILE, LANES), F32)],
        compiler_params=_params("arbitrary"),
        name="attn_merge",
    )(o1, l1, view(o4, d4), view(l4, d4), view(o16, d16), view(l16, d16))


def _mix_decode_kernel(u_ref, c_ref, sp_ref, sc_ref, pw_ref, ps_ref, cw_ref, cb_ref, cg_ref, cbb_ref,
                       ya_ref, yc_ref, glu_ref):
    nb = DEC_BATCH
    u = u_ref[0:nb, :]
    lane = lax.broadcasted_iota(jnp.int32, (nb, POOL_WIDTH), 1)
    run = u
    sums = []
    for back in range(1, max(POOL_WINDOWS)):
        run = run + sp_ref[POOL_BUF - back]
        if back + 1 in POOL_WINDOWS:
            sums.append(run)
    wsel = _pool_select(lane, [jnp.full((nb, POOL_WIDTH), float(w), F32) for w in POOL_WINDOWS])
    d = _pool_select(lane, sums) / wsel - u
    ya_ref[...] = jnp.zeros_like(ya_ref)
    ya_ref[0:nb, :] = _bdot(d, pw_ref[...]) * ps_ref[...]

    c = c_ref[0:nb, :]
    glu = c[:, :CONV_WIDTH] * jax.nn.sigmoid(c[:, CONV_WIDTH:])
    glu_ref[...] = glu
    acc = glu * cw_ref[CONV_KERNEL - 1:CONV_KERNEL, :]
    for j in range(CONV_BUF):
        acc = acc + sc_ref[j] * cw_ref[j:j + 1, :]
    y = _layer_norm(acc + cb_ref[...], cg_ref[...], cbb_ref[...])
    yc_ref[...] = jnp.zeros_like(yc_ref)
    yc_ref[0:nb, :] = y * jax.nn.sigmoid(y)


def _mix_decode(u_pool, c_in, sp_t, sc_t, w_bd, scale, conv_w, conv_b, ln_g, ln_b):
    last = N_TOK_TILES - 1
    full = lambda shape: pl.BlockSpec(shape, lambda i: (0,) * len(shape))
    return pl.pallas_call(
        _mix_decode_kernel,
        grid=(1,),
        in_specs=[pl.BlockSpec((TOK_TILE, POOL_WIDTH), lambda i: (last, 0)),
                  pl.BlockSpec((TOK_TILE, 2 * CONV_WIDTH), lambda i: (last, 0)),
                  full((POOL_BUF, DEC_BATCH, POOL_WIDTH)), full((CONV_BUF, DEC_BATCH, CONV_WIDTH)),
                  full((POOL_WIDTH, POOL_WIDTH)), full((1, POOL_WIDTH)),
                  full((CONV_KERNEL, CONV_WIDTH)), full((1, CONV_WIDTH)), full((1, CONV_WIDTH)),
                  full((1, CONV_WIDTH))],
        out_specs=[full((TOK_TILE, POOL_WIDTH)), full((TOK_TILE, CONV_WIDTH)), full((DEC_BATCH, CONV_WIDTH))],
        out_shape=[jax.ShapeDtypeStruct((TOK_TILE, POOL_WIDTH), F32),
                   jax.ShapeDtypeStruct((TOK_TILE, CONV_WIDTH), F32),
                   jax.ShapeDtypeStruct((DEC_BATCH, CONV_WIDTH), F32)],
        compiler_params=_params("arbitrary"),
        name="mix_decode",
    )(u_pool, c_in, sp_t, sc_t, w_bd, scale, conv_w, conv_b, ln_g, ln_b)


def _attn_decode_tables():
    slopes = _alibi_slopes(ATTN_HEADS)
    pos = np.arange(CACHE_LEN)
    dist = (CACHE_LEN - pos).astype(np.float32)
    bias = np.zeros((8, CACHE_LEN), np.float32)
    for h in range(ATTN_HEADS):
        bias[h] = -slopes[h] * dist
    mult = np.zeros((1, CACHE_LEN), np.float32)
    for w, d in DILATED:
        mult[0] += ((CACHE_LEN - pos) % d == 0) & (CACHE_LEN - pos <= w)
    return bias, mult


def _round_bf16(x):
    return x.astype(BF16).astype(F32)


def _attn_decode_kernel(qkv_ref, k_ref, v_ref, bias_ref, mult_ref, y_ref):
    scale = HEAD_DIM ** -0.5
    mult = mult_ref[...]
    n_branch = float(len(DILATED))
    for h in range(ATTN_HEADS):
        q = _round_bf16(qkv_ref[0, h])
        k_new = _round_bf16(qkv_ref[1, h])
        v_new = _round_bf16(qkv_ref[2, h])
        s = jnp.sum(_round_bf16(k_ref[h]) * q, axis=0, keepdims=True) * scale + bias_ref[h:h + 1, :]
        s = jnp.where(mult > 0.0, s, NEG_BIG)
        s_self = jnp.sum(q * k_new, axis=0, keepdims=True) * scale
        m = jnp.maximum(jnp.max(s, axis=1, keepdims=True), s_self)
        p = mult * jnp.exp(s - m)
        p_self = n_branch * jnp.exp(s_self - m)
        l = jnp.sum(p, axis=1, keepdims=True) + p_self
        acc = jnp.sum(_round_bf16(v_ref[h]) * _round_bf16(p), axis=1, keepdims=True) + _round_bf16(p_self) * v_new
        y_ref[h] = acc / l


def _attn_decode(qkv_cols, cache_k_t, cache_v_t, bias, mult, layer):
    cache = pl.BlockSpec((None, None, ATTN_HEADS, HEAD_DIM, CACHE_LEN), lambda b: (layer, b, 0, 0, 0))
    return pl.pallas_call(
        _attn_decode_kernel,
        grid=(DEC_BATCH,),
        in_specs=[pl.BlockSpec((None, 3, ATTN_HEADS, HEAD_DIM, 1), lambda b: (b, 0, 0, 0, 0)),
                  cache, cache,
                  pl.BlockSpec((8, CACHE_LEN), lambda b: (0, 0)),
                  pl.BlockSpec((1, CACHE_LEN), lambda b: (0, 0))],
        out_specs=pl.BlockSpec((None, ATTN_HEADS, HEAD_DIM, 1), lambda b: (b, 0, 0, 0)),
        out_shape=jax.ShapeDtypeStruct((DEC_BATCH, ATTN_HEADS, HEAD_DIM, 1), F32),
        compiler_params=_params("arbitrary"),
        name="attn_decode",
    )(qkv_cols, cache_k_t, cache_v_t, bias, mult)


ROW_CHUNKS = D_MODEL // LANES


def _store_row_tiles(ref, rows):
    n = rows.shape[0]
    for c in range(ROW_CHUNKS):
        ref[pl.ds(c, n, stride=ROW_CHUNKS), :] = rows[:, c * LANES:(c + 1) * LANES]


def _load_row_tiles(ref, n):
    return jnp.concatenate([ref[pl.ds(c, n, stride=ROW_CHUNKS), :] for c in range(ROW_CHUNKS)], axis=1)


def _post_mix_kernel(x_ref, ya_ref, yb_ref, yc_ref, yad_ref, ybd_ref, ycd_ref, wo_ref, g_ref, b_ref, rw_ref, rb_ref,
                     tri_ref, x1_ref, x1t_ref, info_ref, gate_ref, cnt_ref, wob_ref, run_ref):
    i = pl.program_id(0)

    @pl.when(i == 0)
    def _():
        wob_ref[...] = wo_ref[...].astype(BF16)
        run_ref[...] = jnp.zeros_like(run_ref)

    is_decode = i == N_TOK_TILES - 1
    mixed = jnp.concatenate([jnp.where(is_decode, d_ref[...], p_ref[...]) for p_ref, d_ref in
                             ((ya_ref, yad_ref), (yb_ref, ybd_ref), (yc_ref, ycd_ref))], axis=1)
    mix = jnp.dot(mixed.astype(BF16), wob_ref[...], preferred_element_type=F32)
    x1 = _layer_norm(DEEPNORM_ALPHA * x_ref[...] + mix, g_ref[...], b_ref[...])
    x1_ref[...] = x1
    _store_row_tiles(x1t_ref, x1)
    logits = _bdot(x1, rw_ref[...]) + rb_ref[...]
    lane = lax.broadcasted_iota(jnp.int32, (TOK_TILE, LANES), 1).astype(F32)
    work = logits
    chosen = jnp.zeros((TOK_TILE, LANES), F32)
    tops, idxs, hots = [], [], []
    for _ in range(TOP_K):
        mk = jnp.max(work, axis=-1, keepdims=True)
        idx = jnp.min(jnp.where(work == mk, lane, float(LANES)), axis=-1, keepdims=True)
        hot = lane == idx
        tops.append(mk)
        idxs.append(idx)
        hots.append(hot)
        chosen = chosen + jnp.where(hot, 1.0, 0.0)
        work = jnp.where(hot, -jnp.inf, work)
    exps = [jnp.exp(v - tops[0]) for v in tops]
    denom = exps[0] + exps[1] + exps[2] + exps[3]
    ahead = jnp.dot(tri_ref[...], chosen.astype(BF16), preferred_element_type=F32) + run_ref[0:1, :]
    ranks = [jnp.sum(jnp.where(hot, ahead, 0.0), axis=-1, keepdims=True) for hot in hots]
    info = jnp.zeros((TOK_TILE, LANES), F32)
    gates = jnp.zeros((TOK_TILE, LANES), F32)
    for k in range(TOP_K):
        info = jnp.where(lane == float(k), idxs[k], info)
        info = jnp.where(lane == float(TOP_K + k), ranks[k], info)
        gates = jnp.where(lane == float(k), exps[k] / denom, gates)
    info_ref[...] = info.astype(jnp.int32)
    gate_ref[...] = gates
    run_ref[...] = run_ref[...] + jnp.sum(chosen, axis=0, keepdims=True)
    cnt_ref[...] = run_ref[...]


def _post_mix(x, mix_prompt, mix_decode, w_o, ln_g, ln_b, router_w, router_b, tri, layer):
    row = pl.BlockSpec((TOK_TILE, D_MODEL), lambda i: (i, 0))
    vec = pl.BlockSpec((1, D_MODEL), lambda i: (0, 0))
    lanes = pl.BlockSpec((TOK_TILE, LANES), lambda i: (i, 0))
    widths = (POOL_WIDTH, ATTN_WIDTH, CONV_WIDTH)
    prompt = [pl.BlockSpec((TOK_TILE, w), lambda i: (jnp.minimum(i, N_PROMPT_TILES - 1), 0)) for w in widths]
    decode = [pl.BlockSpec((TOK_TILE, w), lambda i: (0, 0)) for w in widths]
    return pl.pallas_call(
        _post_mix_kernel,
        grid=(N_TOK_TILES,),
        in_specs=[row] + prompt + decode + [
                  pl.BlockSpec((None, D_MODEL, D_MODEL), lambda i: (layer, 0, 0), pipeline_mode=pl.Buffered(1)),
                  vec, vec,
                  pl.BlockSpec((D_MODEL, LANES), lambda i: (0, 0)),
                  pl.BlockSpec((1, LANES), lambda i: (0, 0)),
                  pl.BlockSpec((TOK_TILE, TOK_TILE), lambda i: (0, 0))],
        out_specs=[row, pl.BlockSpec((TOK_TILE * ROW_CHUNKS, LANES), lambda i: (i, 0)), lanes, lanes,
                   pl.BlockSpec((8, LANES), lambda i: (0, 0))],
        out_shape=[jax.ShapeDtypeStruct((M_TOT, D_MODEL), F32),
                   jax.ShapeDtypeStruct((M_TOT * ROW_CHUNKS, LANES), F32),
                   jax.ShapeDtypeStruct((M_TOT, LANES), jnp.int32),
                   jax.ShapeDtypeStruct((M_TOT, LANES), F32),
                   jax.ShapeDtypeStruct((8, LANES), F32)],
        scratch_shapes=[pltpu.VMEM((D_MODEL, D_MODEL), BF16), pltpu.VMEM((8, LANES), F32)],
        compiler_params=_params("arbitrary"),
        name="post_mix",
    )(x, *mix_prompt, *mix_decode, w_o, ln_g, ln_b, router_w, router_b, tri)


def _dispatch_kernel(dest_ref, fill_ref, nu_ref, x_ref, xs_hbm, zero_ref, sem):
    i = pl.program_id(0)

    @pl.when(i == 0)
    def _():
        zero_ref[...] = jnp.zeros_like(zero_ref)

        def tile_copy(start):
            return pltpu.make_async_copy(zero_ref, xs_hbm.at[pl.ds(start, EXPERT_TILE)], sem)

        def over_partial_tiles(act):
            def body(e, carry):
                @pl.when(fill_ref[e] >= 0)
                def _():
                    act(tile_copy(fill_ref[e]))
                return carry
            lax.fori_loop(0, N_EXPERTS, body, 0)

        def over_unused_tiles(act):
            def body(tile, carry):
                act(tile_copy(tile * EXPERT_TILE))
                return carry
            lax.fori_loop(nu_ref[0], N_EXPERT_TILES, body, 0)

        over_partial_tiles(lambda c: c.start())
        over_unused_tiles(lambda c: c.start())
        over_partial_tiles(lambda c: c.wait())
        over_unused_tiles(lambda c: c.wait())

    base = i * (TOK_TILE * TOP_K)
    for r in range(TOK_TILE):
        for k in range(TOP_K):
            pltpu.make_async_copy(x_ref.at[r], xs_hbm.at[dest_ref[base + r * TOP_K + k]], sem).start(priority=k % 2)
    for _ in range(TOP_K):
        pltpu.make_async_copy(x_ref, xs_hbm.at[pl.ds(0, TOK_TILE)], sem).wait()


def _dispatch(dest_flat, fill_start, n_used, x1_tiles):
    return pl.pallas_call(
        _dispatch_kernel,
        grid_spec=pltpu.PrefetchScalarGridSpec(
            num_scalar_prefetch=3,
            grid=(N_TOK_TILES,),
            in_specs=[pl.BlockSpec((TOK_TILE, ROW_CHUNKS, LANES), lambda i, d, f, n: (i, 0, 0))],
            out_specs=pl.BlockSpec(memory_space=pl.ANY),
            scratch_shapes=[pltpu.VMEM((EXPERT_TILE, ROW_CHUNKS, LANES), F32), pltpu.SemaphoreType.DMA]),
        out_shape=jax.ShapeDtypeStruct((N_SORTED, ROW_CHUNKS, LANES), F32),
        compiler_params=_params("arbitrary"),
        name="moe_dispatch",
    )(dest_flat, fill_start, n_used, x1_tiles)


def _experts_kernel(layer, te_ref, nu_ref, first_ref, next_ref, slot_ref, x_ref, w1_hbm, b1_ref, w2_hbm, b2_ref,
                    o_ref, w1f_ref, w2f_ref, w1b_ref, w2b_ref, sem):
    i = pl.program_id(0)

    def weight_copies(e, slot):
        return (pltpu.make_async_copy(w1_hbm.at[layer, e], w1f_ref.at[slot], sem.at[0, slot]),
                pltpu.make_async_copy(w2_hbm.at[layer, e], w2f_ref.at[slot], sem.at[1, slot]))

    @pl.when(i == 0)
    def _():
        for c in weight_copies(te_ref[0], 0):
            c.start()

    @pl.when(first_ref[i] == 1)
    def _():
        slot = slot_ref[i]
        for c in weight_copies(te_ref[i], slot):
            c.wait()
        w1b_ref[...] = w1f_ref[slot].astype(BF16)
        w2b_ref[...] = w2f_ref[slot].astype(BF16)

        @pl.when(next_ref[i] >= 0)
        def _():
            for c in weight_copies(next_ref[i], 1 - slot):
                c.start()

    @pl.when(i < nu_ref[0])
    def _():
        x = _load_row_tiles(x_ref, EXPERT_TILE)
        h = jnp.dot(x.astype(BF16), w1b_ref[...], preferred_element_type=F32) + b1_ref[...]
        gate = jnp.minimum(h[:, :D_EXPERT], SWIGLU_LIMIT)
        up = jnp.clip(h[:, D_EXPERT:], -SWIGLU_LIMIT, SWIGLU_LIMIT)
        act = gate * jax.nn.sigmoid(SWIGLU_ALPHA * gate) * (up + 1.0)
        _store_row_tiles(o_ref, jnp.dot(act.astype(BF16), w2b_ref[...], preferred_element_type=F32) + b2_ref[...])

    @pl.when(i >= nu_ref[0])
    def _():
        o_ref[...] = jnp.zeros_like(o_ref)


def _experts(tables, xs, w1, b1, w2, b2, layer):
    bias = lambda n: pl.BlockSpec((None, None, 1, n), lambda i, te, *_: (layer, te[i], 0, 0))
    rows = pl.BlockSpec((EXPERT_TILE * ROW_CHUNKS, LANES), lambda i, te, nu, *_: (jnp.minimum(i, nu[0] - 1), 0))
    hbm = pl.BlockSpec(memory_space=pl.ANY)
    return pl.pallas_call(
        functools.partial(_experts_kernel, layer),
        grid_spec=pltpu.PrefetchScalarGridSpec(
            num_scalar_prefetch=len(tables),
            grid=(N_EXPERT_TILES,),
            in_specs=[rows, hbm, bias(2 * D_EXPERT), hbm, bias(D_MODEL)],
            out_specs=pl.BlockSpec((EXPERT_TILE * ROW_CHUNKS, LANES), lambda i, *_: (i, 0)),
            scratch_shapes=[pltpu.VMEM((2, D_MODEL, 2 * D_EXPERT), F32), pltpu.VMEM((2, D_EXPERT, D_MODEL), F32),
                            pltpu.VMEM((D_MODEL, 2 * D_EXPERT), BF16), pltpu.VMEM((D_EXPERT, D_MODEL), BF16),
                            pltpu.SemaphoreType.DMA((2, 2))]),
        out_shape=jax.ShapeDtypeStruct((N_SORTED * ROW_CHUNKS, LANES), F32),
        compiler_params=_params("arbitrary"),
        name="moe_experts",
    )(*tables, xs, w1, b1, w2, b2)


def _expert_tables(tile_expert, n_used, counts):
    tile = jnp.arange(N_EXPERT_TILES, dtype=jnp.int32)
    opens = (tile < n_used[0]) & ((tile == 0) | (tile_expert != jnp.roll(tile_expert, 1)))
    slot = (jnp.cumsum(opens.astype(jnp.int32)) - 1) % 2
    present = jnp.where(counts[0, :N_EXPERTS] > 0, jnp.arange(N_EXPERTS, dtype=jnp.int32), N_EXPERTS)
    after = jnp.concatenate([lax.cummin(present, reverse=True)[1:], jnp.full((1,), N_EXPERTS, jnp.int32)])
    next_expert = jnp.where(after < N_EXPERTS, after, -1)[tile_expert]
    return (tile_expert, n_used, opens.astype(jnp.int32), next_expert.astype(jnp.int32), slot.astype(jnp.int32))


def _combine_kernel(dest_ref, x1_ref, gate_ref, g_ref, b_ref, ys_hbm, o_ref, rows_ref, sem):
    i = pl.program_id(0)

    def gather(tile, slot):
        base = tile * (TOK_TILE * TOP_K)
        for r in range(TOK_TILE):
            for k in range(TOP_K):
                pltpu.make_async_copy(ys_hbm.at[dest_ref[base + r * TOP_K + k]],
                                      rows_ref.at[slot, k, pl.ds(r * ROW_CHUNKS, ROW_CHUNKS), :],
                                      sem.at[slot]).start(priority=k % 2)

    @pl.when(i == 0)
    def _():
        gather(0, 0)

    for parity in range(2):
        @pl.when((i + 1 < N_TOK_TILES) & (i % 2 == parity))
        def _():
            gather(i + 1, 1 - parity)

    slot = i % 2
    for _ in range(TOP_K):
        pltpu.make_async_copy(ys_hbm.at[pl.ds(0, TOK_TILE)], ys_hbm.at[pl.ds(0, TOK_TILE)], sem.at[slot]).wait()
    gates = gate_ref[...]
    moe = gates[:, 0:1] * _load_row_tiles(rows_ref.at[slot, 0], TOK_TILE)
    for k in range(1, TOP_K):
        moe = moe + gates[:, k:k + 1] * _load_row_tiles(rows_ref.at[slot, k], TOK_TILE)
    o_ref[...] = _layer_norm(DEEPNORM_ALPHA * x1_ref[...] + moe, g_ref[...], b_ref[...])


def _combine(dest_flat, x1, gates, ln_g, ln_b, ys_tiles):
    row = pl.BlockSpec((TOK_TILE, D_MODEL), lambda i, d: (i, 0))
    vec = pl.BlockSpec((1, D_MODEL), lambda i, d: (0, 0))
    return pl.pallas_call(
        _combine_kernel,
        grid_spec=pltpu.PrefetchScalarGridSpec(
            num_scalar_prefetch=1,
            grid=(N_TOK_TILES,),
            in_specs=[row, pl.BlockSpec((TOK_TILE, LANES), lambda i, d: (i, 0)), vec, vec,
                      pl.BlockSpec(memory_space=pl.ANY)],
            out_specs=row,
            scratch_shapes=[pltpu.VMEM((2, TOP_K, TOK_TILE * ROW_CHUNKS, LANES), F32),
                            pltpu.SemaphoreType.DMA((2,))]),
        out_shape=jax.ShapeDtypeStruct((M_TOT, D_MODEL), F32),
        compiler_params=_params("arbitrary"),
        name="moe_combine",
    )(dest_flat, x1, gates, ln_g, ln_b, ys_tiles)


def _routing_tables(info, counts):
    top_i = info[:, :TOP_K]
    rank = info[:, TOP_K:2 * TOP_K]
    cnt = counts[0, :N_EXPERTS].astype(jnp.int32)
    padded = (cnt + EXPERT_TILE - 1) // EXPERT_TILE * EXPERT_TILE
    pad_end = jnp.cumsum(padded)
    pad_start = pad_end - padded
    hot = top_i[:, :, None] == jnp.arange(N_EXPERTS, dtype=jnp.int32)
    dest = rank + jnp.sum(jnp.where(hot, pad_start, 0), axis=-1)
    fill_start = jnp.where(cnt % EXPERT_TILE != 0, pad_end - EXPERT_TILE, -1).astype(jnp.int32)
    tile_start = jnp.arange(N_EXPERT_TILES, dtype=jnp.int32) * EXPERT_TILE
    n_used = (pad_end[-1] // EXPERT_TILE).astype(jnp.int32)
    tile_expert = jnp.sum(tile_start[:, None] >= pad_end[None, :], axis=1).astype(jnp.int32)
    last_expert = tile_expert[jnp.maximum(n_used - 1, 0)]
    tile_expert = jnp.where(tile_start < pad_end[-1], tile_expert, last_expert)
    return dest.reshape(-1).astype(jnp.int32), fill_start, tile_expert, n_used.reshape(1)


def _block_diag(pool_w):
    g, n, _ = pool_w.shape
    out = jnp.zeros((g * n, g * n), pool_w.dtype)
    for i in range(g):
        out = out.at[i * n:(i + 1) * n, i * n:(i + 1) * n].set(pool_w[i])
    return out


def kernel(x_prompt, x_sample, state_pool, state_conv, cache_attn_k, cache_attn_v, w_in, pool_w, pool_scale, conv_w, conv_b, conv_ln_g, conv_ln_b, w_o, ln1_g, ln1_b, router_w, router_b, expert_w_in, expert_b_in, expert_w_out, expert_b_out, ln2_g, ln2_b):
    bias_prompt = jnp.asarray(_attn_bias_tables())
    bias_decode, mult_decode = (jnp.asarray(t) for t in _attn_decode_tables())
    tri = jnp.asarray(np.tril(np.ones((TOK_TILE, TOK_TILE), np.float32), -1), BF16)
    x = jnp.concatenate([x_prompt.reshape(M_PROMPT, D_MODEL), x_sample.reshape(DEC_BATCH, D_MODEL),
                         jnp.zeros((M_TOT - M_PROMPT - DEC_BATCH, D_MODEL), F32)], axis=0)
    cache_k_t = cache_attn_k.transpose(0, 1, 3, 4, 2)
    cache_v_t = cache_attn_v.transpose(0, 1, 3, 4, 2)
    b_in = expert_b_in.reshape(DEPTH, N_EXPERTS, 1, 2 * D_EXPERT)
    b_out = expert_b_out.reshape(DEPTH, N_EXPERTS, 1, D_MODEL)
    outs = {name: [] for name in ("p_pool", "p_conv", "p_k", "p_v", "s_pool", "s_conv", "s_k", "s_v")}
    for l in range(DEPTH):
        u_pool, c_in, qkv_d, q1, q4, q16, kv_t = _in_proj(x, w_in, l)
        w_bd = _block_diag(pool_w[l])
        scale = pool_scale[l].reshape(1, POOL_WIDTH)
        cb, cg, cbb = (v[l].reshape(1, CONV_WIDTH) for v in (conv_b, conv_ln_g, conv_ln_b))

        ya = _pool_prompt(u_pool, w_bd, scale)
        yc, glu = _conv_prompt(c_in, conv_w[l], cb, cg, cbb)
        branches = [_attn_prompt(q.reshape(-1, QKV_WIDTH), bias_prompt[bi], d)
                    for bi, (q, (_, d)) in enumerate(zip((q1, q4, q16), DILATED))]
        yb = _attn_merge(branches)

        sp_t = state_pool[l].transpose(1, 0, 2)
        sc_t = state_conv[l].transpose(1, 0, 2)
        ya_d, yc_d, glu_d = _mix_decode(u_pool, c_in, sp_t, sc_t, w_bd, scale, conv_w[l], cb, cg, cbb)
        qkv_d = qkv_d[:DEC_BATCH]
        yb_d = _attn_decode(qkv_d.reshape(DEC_BATCH, 3, ATTN_HEADS, HEAD_DIM, 1), cache_k_t, cache_v_t,
                            bias_decode, mult_decode, l)
        yb_d = jnp.pad(yb_d.reshape(DEC_BATCH, ATTN_WIDTH), ((0, TOK_TILE - DEC_BATCH), (0, 0)))

        rw = jnp.pad(router_w[l], ((0, 0), (0, LANES - N_EXPERTS)))
        rb = jnp.pad(router_b[l], (0, LANES - N_EXPERTS), constant_values=NEG_BIG).reshape(1, LANES)
        x1, x1_tiles, info, gates, counts = _post_mix(x, (ya, yb, yc), (ya_d, yb_d, yc_d), w_o,
                                                      ln1_g[l].reshape(1, D_MODEL), ln1_b[l].reshape(1, D_MODEL),
                                                      rw, rb, tri, l)

        dest, fill_start, tile_expert, n_used = _routing_tables(info, counts)
        xs = _dispatch(dest, fill_start, n_used, x1_tiles.reshape(M_TOT, ROW_CHUNKS, LANES))
        ys = _experts(_expert_tables(tile_expert, n_used, counts), xs.reshape(N_SORTED * ROW_CHUNKS, LANES),
                      expert_w_in, b_in, expert_w_out, b_out, l)
        x = _combine(dest, x1, gates, ln2_g[l].reshape(1, D_MODEL), ln2_b[l].reshape(1, D_MODEL),
                     ys.reshape(N_SORTED, ROW_CHUNKS, LANES))

        kv_t = kv_t[:BATCH].reshape(BATCH, 2, ATTN_HEADS, HEAD_DIM, CACHE_LEN).transpose(1, 0, 4, 2, 3)
        last_rows = lambda a, n: a[:M_PROMPT].reshape(BATCH, SEQ, a.shape[-1])[:, SEQ - n:]
        outs["p_pool"].append(last_rows(u_pool, POOL_BUF))
        outs["p_conv"].append(last_rows(glu, CONV_BUF))
        outs["p_k"].append(kv_t[0])
        outs["p_v"].append(kv_t[1])
        u_d = u_pool[M_PROMPT:M_PROMPT + DEC_BATCH]
        outs["s_pool"].append(jnp.concatenate([state_pool[l][:, 1:], u_d[:, None]], axis=1))
        outs["s_conv"].append(jnp.concatenate([state_conv[l][:, 1:], glu_d[:, None]], axis=1))
        outs["s_k"].append(qkv_d[:, ATTN_WIDTH:2 * ATTN_WIDTH].reshape(DEC_BATCH, 1, ATTN_HEADS, HEAD_DIM))
        outs["s_v"].append(qkv_d[:, 2 * ATTN_WIDTH:].reshape(DEC_BATCH, 1, ATTN_HEADS, HEAD_DIM))

    y_p = x[:M_PROMPT].reshape(BATCH, SEQ, D_MODEL)
    y_s = x[M_PROMPT:M_PROMPT + DEC_BATCH].reshape(DEC_BATCH, 1, D_MODEL)
    return (y_p, y_s) + tuple(jnp.stack(outs[name]) for name in
                              ("p_pool", "p_conv", "p_k", "p_v", "s_pool", "s_conv", "s_k", "s_v"))
```

```python
import functools
import math

import numpy as np
import jax
import jax.numpy as jnp
from jax import lax
from jax.experimental import pallas as pl
from jax.experimental.pallas import tpu as pltpu

F32 = jnp.float32
BF16 = jnp.bfloat16

D_MODEL = 1024
BATCH = 4
SEQ = 4096
DEPTH = 4
DEC_BATCH = 32
PAST_LEN = 8192
POOL_WIDTH = 256
POOL_WINDOWS = (2, 4, 8, 16)
POOL_BUF = 15
ATTN_HEADS = 6
HEAD_DIM = 64
ATTN_WIDTH = ATTN_HEADS * HEAD_DIM
DILATED = ((128, 1), (512, 4), (2048, 16))
ATTN_BLOCK = 128
CACHE_LEN = 2048
CONV_WIDTH = 384
CONV_KERNEL = 31
CONV_BUF = 30
N_EXPERTS = 32
TOP_K = 4
D_EXPERT = 1024
SWIGLU_ALPHA = 1.702
SWIGLU_LIMIT = 7.0
DEEPNORM_ALPHA = (2 * DEPTH) ** 0.25
LN_EPS = 1e-5
NEG_BIG = -1e30
LANES = 128

M_PROMPT = BATCH * SEQ
TOK_TILE = 256
M_TOT = M_PROMPT + TOK_TILE
N_TOK_TILES = M_TOT // TOK_TILE
SEQ_TILE = 512
ATTN_Q_TILE = 2048
EXPERT_TILE = 512
N_ASSIGN = M_TOT * TOP_K
N_EXPERT_TILES = -(-N_ASSIGN // EXPERT_TILE) + N_EXPERTS
N_SORTED = N_EXPERT_TILES * EXPERT_TILE
VMEM_LIMIT = 56 * 1024 * 1024


def _alibi_slopes(n):
    def pow2(k):
        start = 2.0 ** (-8.0 / k)
        return [start ** (i + 1) for i in range(k)]
    if math.log2(n).is_integer():
        return pow2(n)
    c = 2 ** math.floor(math.log2(n))
    return pow2(c) + pow2(2 * c)[0::2][: n - c]


def _bdot(a, b):
    return jnp.dot(a.astype(BF16), b.astype(BF16), preferred_element_type=F32)


def _layer_norm(x, g, b):
    mu = jnp.mean(x, axis=-1, keepdims=True)
    xc = x - mu
    var = jnp.mean(xc * xc, axis=-1, keepdims=True)
    return xc * lax.rsqrt(var + LN_EPS) * g + b


def _params(*sem):
    return pltpu.CompilerParams(dimension_semantics=sem, vmem_limit_bytes=VMEM_LIMIT)


N_PROMPT_TILES = M_PROMPT // TOK_TILE
TILES_PER_SEQ = SEQ // TOK_TILE
QKV_WIDTH = 3 * ATTN_WIDTH
QKV_CHUNKS = QKV_WIDTH // LANES


def _in_proj_body(x, w_ref, wb_ref, stage_ref, pool_ref, conv_ref, qkvd_ref, q1_ref, q4_ref, q16_ref, kvt_ref):
    @pl.when(pl.program_id(0) == 0)
    def _():
        wb_ref[...] = w_ref[...].astype(BF16)

    res = jnp.dot(x.astype(BF16), wb_ref[...], preferred_element_type=F32)
    qkv = res[:, POOL_WIDTH:POOL_WIDTH + QKV_WIDTH]
    pool_ref[...] = res[:, :POOL_WIDTH]
    conv_ref[...] = res[:, POOL_WIDTH + QKV_WIDTH:]
    qkvd_ref[...] = qkv
    kvt_ref[...] = jnp.transpose(qkv[:, ATTN_WIDTH:])
    q1_ref[...] = qkv.astype(BF16)
    for c in range(QKV_CHUNKS):
        stage_ref[c] = qkv[:, c * LANES:(c + 1) * LANES]
    for ref, d in ((q4_ref, DILATED[1][1]), (q16_ref, DILATED[2][1])):
        n = TOK_TILE // d
        for r in range(d):
            rows = [stage_ref[c, pl.ds(r, n, stride=d), :] for c in range(QKV_CHUNKS)]
            ref[r] = jnp.concatenate(rows, axis=1).astype(BF16)


def _in_proj_kernel(x_ref, w_ref, pool_ref, conv_ref, qkvd_ref, q1_ref, q4_ref, q16_ref, kvt_ref, wb_ref, stage_ref):
    _in_proj_body(x_ref[...], w_ref, wb_ref, stage_ref, pool_ref, conv_ref, qkvd_ref, q1_ref, q4_ref, q16_ref, kvt_ref)


CACHE_TILE0 = (SEQ - CACHE_LEN) // TOK_TILE
N_IN = POOL_WIDTH + QKV_WIDTH + 2 * CONV_WIDTH


def _step_tile(s):
    return (s + N_TOK_TILES - 1) % N_TOK_TILES


def _in_proj_outputs():
    d4, d16 = DILATED[1][1], DILATED[2][1]
    ptile = lambda s: jnp.maximum(s - 1, 0)
    by_residue = lambda d: pl.BlockSpec(
        (None, d, TOK_TILE // d, QKV_WIDTH),
        lambda s, *_: (ptile(s) // TILES_PER_SEQ, 0, ptile(s) % TILES_PER_SEQ, 0))
    specs = [pl.BlockSpec((TOK_TILE, POOL_WIDTH), lambda s, *_: (_step_tile(s), 0)),
             pl.BlockSpec((TOK_TILE, 2 * CONV_WIDTH), lambda s, *_: (_step_tile(s), 0)),
             pl.BlockSpec((TOK_TILE, QKV_WIDTH), lambda s, *_: (jnp.minimum(s, 1), 0)),
             pl.BlockSpec((TOK_TILE, QKV_WIDTH), lambda s, *_: (_step_tile(s), 0)),
             by_residue(d4), by_residue(d16),
             pl.BlockSpec((None, 2 * ATTN_WIDTH, TOK_TILE),
                          lambda s, *_: (ptile(s) // TILES_PER_SEQ, 0,
                                         jnp.maximum(ptile(s) % TILES_PER_SEQ - CACHE_TILE0, 0)))]
    shapes = [jax.ShapeDtypeStruct((M_TOT, POOL_WIDTH), F32),
              jax.ShapeDtypeStruct((M_TOT, 2 * CONV_WIDTH), F32),
              jax.ShapeDtypeStruct((2 * TOK_TILE, QKV_WIDTH), F32),
              jax.ShapeDtypeStruct((M_TOT, QKV_WIDTH), BF16),
              jax.ShapeDtypeStruct((BATCH, d4, SEQ // d4, QKV_WIDTH), BF16),
              jax.ShapeDtypeStruct((BATCH, d16, SEQ // d16, QKV_WIDTH), BF16),
              jax.ShapeDtypeStruct((BATCH, 2 * ATTN_WIDTH, CACHE_LEN), F32)]
    scratch = [pltpu.VMEM((D_MODEL, N_IN), BF16), pltpu.VMEM((QKV_CHUNKS, TOK_TILE, LANES), F32)]
    return specs, shapes, scratch


def _w_in_spec(layer):
    return pl.BlockSpec((None, D_MODEL, N_IN), lambda s, *_: (layer, 0, 0), pipeline_mode=pl.Buffered(1))


def _in_proj(x, w_in, layer):
    specs, shapes, scratch = _in_proj_outputs()
    return pl.pallas_call(
        _in_proj_kernel,
        grid=(N_TOK_TILES,),
        in_specs=[pl.BlockSpec((TOK_TILE, D_MODEL), lambda s: (_step_tile(s), 0)), _w_in_spec(layer)],
        out_specs=specs,
        out_shape=shapes,
        scratch_shapes=scratch,
        compiler_params=_params("arbitrary"),
        name="in_proj",
    )(x, w_in)


def _pool_select(lane, vals):
    out = vals[-1]
    for g in range(len(vals) - 2, -1, -1):
        out = jnp.where(lane < (g + 1) * 64, vals[g], out)
    return out


def _pool_prompt_kernel(u_ref, w_ref, scale_ref, o_ref, halo_ref):
    t = pl.program_id(1)

    @pl.when(t == 0)
    def _():
        halo_ref[...] = jnp.zeros_like(halo_ref)

    u = u_ref[...]
    n = SEQ_TILE + 16
    ext = jnp.concatenate([halo_ref[...], u], axis=0)
    p2 = ext + pltpu.roll(ext, 1, 0)
    p4 = p2 + pltpu.roll(p2, 2, 0)
    p8 = p4 + pltpu.roll(p4, 4, 0)
    p16 = p8 + pltpu.roll(p8, 8, 0)
    lane = lax.broadcasted_iota(jnp.int32, (SEQ_TILE, POOL_WIDTH), 1)
    row = lax.broadcasted_iota(jnp.int32, (SEQ_TILE, POOL_WIDTH), 0)
    pos1 = (row + t * SEQ_TILE + 1).astype(F32)
    wsel = _pool_select(lane, [jnp.full((SEQ_TILE, POOL_WIDTH), float(w), F32) for w in POOL_WINDOWS])
    cnt = jnp.minimum(wsel, pos1)
    sums = _pool_select(lane, [p[16:n] for p in (p2, p4, p8, p16)])
    d = sums / cnt - u
    o_ref[...] = _bdot(d, w_ref[...]) * scale_ref[...]
    halo_ref[...] = u[SEQ_TILE - 16:]


def _pool_prompt(u_pool, w_bd, scale):
    nt = SEQ // SEQ_TILE
    return pl.pallas_call(
        _pool_prompt_kernel,
        grid=(BATCH, nt),
        in_specs=[pl.BlockSpec((SEQ_TILE, POOL_WIDTH), lambda b, t: (b * nt + t, 0)),
                  pl.BlockSpec((POOL_WIDTH, POOL_WIDTH), lambda b, t: (0, 0)),
                  pl.BlockSpec((1, POOL_WIDTH), lambda b, t: (0, 0))],
        out_specs=pl.BlockSpec((SEQ_TILE, POOL_WIDTH), lambda b, t: (b * nt + t, 0)),
        out_shape=jax.ShapeDtypeStruct((M_PROMPT, POOL_WIDTH), F32),
        scratch_shapes=[pltpu.VMEM((16, POOL_WIDTH), F32)],
        compiler_params=_params("arbitrary", "arbitrary"),
        name="pool_prompt",
    )(u_pool, w_bd, scale)


CONV_HALO = 32


def _conv_prompt_kernel(c_ref, w_ref, b_ref, g_ref, bb_ref, y_ref, u_ref, ext_ref, sh_ref):
    t = pl.program_id(1)

    @pl.when(t == 0)
    def _():
        ext_ref[0:CONV_HALO, :] = jnp.zeros((CONV_HALO, CONV_WIDTH), F32)

    c = c_ref[...]
    u = c[:, :CONV_WIDTH] * jax.nn.sigmoid(c[:, CONV_WIDTH:])
    u_ref[...] = u
    ext_ref[CONV_HALO:CONV_HALO + SEQ_TILE, :] = u
    span = SEQ_TILE + CONV_HALO - 8
    for ph in range(1, 8):
        sh_ref[ph, 0:span, :] = ext_ref[pl.ds(ph, span), :]

    acc = jnp.zeros((SEQ_TILE, CONV_WIDTH), F32)
    for j in range(CONV_KERNEL):
        off = j + 2
        a, ph = off // 8, off % 8
        tap = ext_ref[8 * a:8 * a + SEQ_TILE, :] if ph == 0 else sh_ref[ph, 8 * a:8 * a + SEQ_TILE, :]
        acc = acc + tap * w_ref[j:j + 1, :]
    y = _layer_norm(acc + b_ref[...], g_ref[...], bb_ref[...])
    y_ref[...] = y * jax.nn.sigmoid(y)
    ext_ref[0:CONV_HALO, :] = u[SEQ_TILE - CONV_HALO:]


def _conv_prompt(c_in, conv_w, conv_b, ln_g, ln_b):
    nt = SEQ // SEQ_TILE
    vec = pl.BlockSpec((1, CONV_WIDTH), lambda b, t: (0, 0))
    return pl.pallas_call(
        _conv_prompt_kernel,
        grid=(BATCH, nt),
        in_specs=[pl.BlockSpec((SEQ_TILE, 2 * CONV_WIDTH), lambda b, t: (b * nt + t, 0)),
                  pl.BlockSpec((CONV_KERNEL, CONV_WIDTH), lambda b, t: (0, 0)),
                  vec, vec, vec],
        out_specs=[pl.BlockSpec((SEQ_TILE, CONV_WIDTH), lambda b, t: (b * nt + t, 0)),
                   pl.BlockSpec((SEQ_TILE, CONV_WIDTH), lambda b, t: (b * nt + t, 0))],
        out_shape=[jax.ShapeDtypeStruct((M_PROMPT, CONV_WIDTH), F32),
                   jax.ShapeDtypeStruct((M_PROMPT, CONV_WIDTH), F32)],
        scratch_shapes=[pltpu.VMEM((SEQ_TILE + CONV_HALO, CONV_WIDTH), F32),
                        pltpu.VMEM((8, SEQ_TILE + CONV_HALO, CONV_WIDTH), F32)],
        compiler_params=_params("arbitrary", "arbitrary"),
        name="conv_prompt",
    )(c_in, conv_w, conv_b, ln_g, ln_b)


def _attn_bias_tables():
    slopes = _alibi_slopes(ATTN_HEADS)
    qi = np.arange(ATTN_BLOCK)[:, None] + ATTN_BLOCK
    ki = np.arange(2 * ATTN_BLOCK)[None, :]
    delta = qi - ki
    tabs = np.zeros((len(DILATED), ATTN_HEADS // 2, 4, ATTN_BLOCK, 2 * ATTN_BLOCK), np.float32)
    for bi, (w, d) in enumerate(DILATED):
        n_back = w // d
        band = (delta >= 0) & (delta <= n_back)
        for h in range(ATTN_HEADS):
            bias = -slopes[h] * (delta * d).astype(np.float32)
            for first in range(2):
                valid = band & (ki >= ATTN_BLOCK) if first else band
                tabs[bi, h // 2, 2 * (h % 2) + first] = np.where(valid, bias, NEG_BIG)
    return tabs


def _attn_prompt_kernel(blocks_per_seq, q_ref, kp_ref, kc_ref, vp_ref, vc_ref, bias_ref, o_ref, lse_ref):
    t = pl.program_id(1)
    kk = jnp.concatenate([kp_ref[...], kc_ref[...]], axis=0)
    vv = jnp.concatenate([vp_ref[...], vc_ref[...]], axis=0)
    lane_q = lax.broadcasted_iota(jnp.int32, (ATTN_BLOCK, 2 * HEAD_DIM), 1)
    for j in range(ATTN_Q_TILE // ATTN_BLOCK):
        blk = t * (ATTN_Q_TILE // ATTN_BLOCK) + j
        first = (blk % blocks_per_seq == 0).astype(jnp.int32)
        q = q_ref[j * ATTN_BLOCK:(j + 1) * ATTN_BLOCK, :]
        keys = kk[j * ATTN_BLOCK:(j + 2) * ATTN_BLOCK]
        vals = vv[j * ATTN_BLOCK:(j + 2) * ATTN_BLOCK]
        outs, lses = [], []
        for h in range(2):
            in_head = (lane_q >= h * HEAD_DIM) & (lane_q < (h + 1) * HEAD_DIM)
            qh = jnp.where(in_head, q, jnp.zeros_like(q))
            s = lax.dot_general(qh, keys, (((1,), (1,)), ((), ())), preferred_element_type=F32)
            s = s * (HEAD_DIM ** -0.5) + bias_ref[2 * h + first]
            m = jnp.max(s, axis=-1, keepdims=True)
            p = jnp.exp(s - m)
            l = jnp.sum(p, axis=-1, keepdims=True)
            acc = jnp.dot(p.astype(BF16), vals, preferred_element_type=F32)
            outs.append(acc / l)
            lses.append(m + jnp.log(l))
        o_ref[j * ATTN_BLOCK:(j + 1) * ATTN_BLOCK, :] = jnp.where(lane_q < HEAD_DIM, outs[0], outs[1])
        lse_ref[j * ATTN_BLOCK:(j + 1) * ATTN_BLOCK, :] = jnp.where(lane_q < HEAD_DIM, lses[0], lses[1])


def _attn_prompt(qkv, bias_tab, dilation):
    n_hp = ATTN_HEADS // 2
    per_tile = ATTN_Q_TILE // ATTN_BLOCK
    nt = M_PROMPT // ATTN_Q_TILE
    width = 2 * HEAD_DIM
    cur = lambda off: pl.BlockSpec((ATTN_Q_TILE, width), lambda hp, t: (t, off * n_hp + hp))
    prev = lambda off: pl.BlockSpec((ATTN_BLOCK, width),
                                    lambda hp, t: (jnp.maximum(t * per_tile - 1, 0), off * n_hp + hp))
    out = pl.BlockSpec((ATTN_Q_TILE, width), lambda hp, t: (t, hp))
    return pl.pallas_call(
        functools.partial(_attn_prompt_kernel, SEQ // dilation // ATTN_BLOCK),
        grid=(n_hp, nt),
        in_specs=[cur(0), prev(1), cur(1), prev(2), cur(2),
                  pl.BlockSpec((None, 4, ATTN_BLOCK, 2 * ATTN_BLOCK), lambda hp, t: (hp, 0, 0, 0))],
        out_specs=[out, out],
        out_shape=[jax.ShapeDtypeStruct((M_PROMPT, ATTN_WIDTH), F32),
                   jax.ShapeDtypeStruct((M_PROMPT, ATTN_WIDTH), F32)],
        compiler_params=_params("arbitrary", "arbitrary"),
        name=f"attn_prompt_d{dilation}",
    )(qkv, qkv, qkv, qkv, qkv, bias_tab)


ATTN_CHUNKS = ATTN_WIDTH // LANES


def _attn_merge_kernel(o1_ref, l1_ref, o4_ref, l4_ref, o16_ref, l16_ref, y_ref, so_ref, sl_ref):
    for j, (o_ref, l_ref, d) in enumerate(((o4_ref, l4_ref, DILATED[1][1]), (o16_ref, l16_ref, DILATED[2][1]))):
        n = TOK_TILE // d
        for r in range(d):
            ov, lv = o_ref[r], l_ref[r]
            for c in range(ATTN_CHUNKS):
                so_ref[j, c, pl.ds(r, n, stride=d), :] = ov[:, c * LANES:(c + 1) * LANES]
                sl_ref[j, c, pl.ds(r, n, stride=d), :] = lv[:, c * LANES:(c + 1) * LANES]
    outs = [o1_ref[...]] + [jnp.concatenate([so_ref[j, c] for c in range(ATTN_CHUNKS)], axis=1) for j in range(2)]
    lses = [l1_ref[...]] + [jnp.concatenate([sl_ref[j, c] for c in range(ATTN_CHUNKS)], axis=1) for j in range(2)]
    top = jnp.maximum(jnp.maximum(lses[0], lses[1]), lses[2])
    ws = [jnp.exp(v - top) for v in lses]
    y_ref[...] = (ws[0] * outs[0] + ws[1] * outs[1] + ws[2] * outs[2]) / (ws[0] + ws[1] + ws[2])


def _attn_merge(branches):
    (o1, l1), (o4, l4), (o16, l16) = branches
    d4, d16 = DILATED[1][1], DILATED[2][1]
    nat = pl.BlockSpec((TOK_TILE, ATTN_WIDTH), lambda i: (i, 0))
    by_residue = lambda d: pl.BlockSpec((None, d, TOK_TILE // d, ATTN_WIDTH),
                                        lambda i: (i // TILES_PER_SEQ, 0, i % TILES_PER_SEQ, 0))
    view = lambda a, d: a.reshape(BATCH, d, SEQ // d, ATTN_WIDTH)
    return pl.pallas_call(
        _attn_merge_kernel,
        grid=(N_PROMPT_TILES,),
        in_specs=[nat, nat, by_residue(d4), by_residue(d4), by_residue(d16), by_residue(d16)],
        out_specs=nat,
        out_shape=jax.ShapeDtypeStruct((M_PROMPT, ATTN_WIDTH), F32),
        scratch_shapes=[pltpu.VMEM((2, ATTN_CHUNKS, TOK_TILE, LANES), F32),
                        pltpu.VMEM((2, ATTN_CHUNKS, TOK_TILE, LANES), F32)],
        compiler_params=_params("arbitrary"),
        name="attn_merge",
    )(o1, l1, view(o4, d4), view(l4, d4), view(o16, d16), view(l16, d16))


def _mix_decode_kernel(u_ref, c_ref, sp_ref, sc_ref, pw_ref, ps_ref, cw_ref, cb_ref, cg_ref, cbb_ref,
                       ya_ref, yc_ref, glu_ref):
    nb = DEC_BATCH
    u = u_ref[0:nb, :]
    lane = lax.broadcasted_iota(jnp.int32, (nb, POOL_WIDTH), 1)
    run = u
    sums = []
    for back in range(1, max(POOL_WINDOWS)):
        run = run + sp_ref[POOL_BUF - back]
        if back + 1 in POOL_WINDOWS:
            sums.append(run)
    wsel = _pool_select(lane, [jnp.full((nb, POOL_WIDTH), float(w), F32) for w in POOL_WINDOWS])
    d = _pool_select(lane, sums) / wsel - u
    ya_ref[...] = jnp.zeros_like(ya_ref)
    ya_ref[0:nb, :] = _bdot(d, pw_ref[...]) * ps_ref[...]

    c = c_ref[0:nb, :]
    glu = c[:, :CONV_WIDTH] * jax.nn.sigmoid(c[:, CONV_WIDTH:])
    glu_ref[...] = glu
    acc = glu * cw_ref[CONV_KERNEL - 1:CONV_KERNEL, :]
    for j in range(CONV_BUF):
        acc = acc + sc_ref[j] * cw_ref[j:j + 1, :]
    y = _layer_norm(acc + cb_ref[...], cg_ref[...], cbb_ref[...])
    yc_ref[...] = jnp.zeros_like(yc_ref)
    yc_ref[0:nb, :] = y * jax.nn.sigmoid(y)


def _mix_decode(u_pool, c_in, sp_t, sc_t, w_bd, scale, conv_w, conv_b, ln_g, ln_b):
    last = N_TOK_TILES - 1
    full = lambda shape: pl.BlockSpec(shape, lambda i: (0,) * len(shape))
    return pl.pallas_call(
        _mix_decode_kernel,
        grid=(1,),
        in_specs=[pl.BlockSpec((TOK_TILE, POOL_WIDTH), lambda i: (last, 0)),
                  pl.BlockSpec((TOK_TILE, 2 * CONV_WIDTH), lambda i: (last, 0)),
                  full((POOL_BUF, DEC_BATCH, POOL_WIDTH)), full((CONV_BUF, DEC_BATCH, CONV_WIDTH)),
                  full((POOL_WIDTH, POOL_WIDTH)), full((1, POOL_WIDTH)),
                  full((CONV_KERNEL, CONV_WIDTH)), full((1, CONV_WIDTH)), full((1, CONV_WIDTH)),
                  full((1, CONV_WIDTH))],
        out_specs=[full((TOK_TILE, POOL_WIDTH)), full((TOK_TILE, CONV_WIDTH)), full((DEC_BATCH, CONV_WIDTH))],
        out_shape=[jax.ShapeDtypeStruct((TOK_TILE, POOL_WIDTH), F32),
                   jax.ShapeDtypeStruct((TOK_TILE, CONV_WIDTH), F32),
                   jax.ShapeDtypeStruct((DEC_BATCH, CONV_WIDTH), F32)],
        compiler_params=_params("arbitrary"),
        name="mix_decode",
    )(u_pool, c_in, sp_t, sc_t, w_bd, scale, conv_w, conv_b, ln_g, ln_b)


def _attn_decode_tables():
    slopes = _alibi_slopes(ATTN_HEADS)
    pos = np.arange(CACHE_LEN)
    dist = (CACHE_LEN - pos).astype(np.float32)
    bias = np.zeros((8, CACHE_LEN), np.float32)
    for h in range(ATTN_HEADS):
        bias[h] = -slopes[h] * dist
    mult = np.zeros((1, CACHE_LEN), np.float32)
    for w, d in DILATED:
        mult[0] += ((CACHE_LEN - pos) % d == 0) & (CACHE_LEN - pos <= w)
    return bias, mult


def _round_bf16(x):
    return x.astype(BF16).astype(F32)


def _attn_decode_kernel(qkv_ref, k_ref, v_ref, bias_ref, mult_ref, y_ref):
    scale = HEAD_DIM ** -0.5
    mult = mult_ref[...]
    n_branch = float(len(DILATED))
    for h in range(ATTN_HEADS):
        q = _round_bf16(qkv_ref[0, h])
        k_new = _round_bf16(qkv_ref[1, h])
        v_new = _round_bf16(qkv_ref[2, h])
        s = jnp.sum(_round_bf16(k_ref[h]) * q, axis=0, keepdims=True) * scale + bias_ref[h:h + 1, :]
        s = jnp.where(mult > 0.0, s, NEG_BIG)
        s_self = jnp.sum(q * k_new, axis=0, keepdims=True) * scale
        m = jnp.maximum(jnp.max(s, axis=1, keepdims=True), s_self)
        p = mult * jnp.exp(s - m)
        p_self = n_branch * jnp.exp(s_self - m)
        l = jnp.sum(p, axis=1, keepdims=True) + p_self
        acc = jnp.sum(_round_bf16(v_ref[h]) * _round_bf16(p), axis=1, keepdims=True) + _round_bf16(p_self) * v_new
        y_ref[h] = acc / l


def _attn_decode(qkv_cols, cache_k_t, cache_v_t, bias, mult, layer):
    cache = pl.BlockSpec((None, None, ATTN_HEADS, HEAD_DIM, CACHE_LEN), lambda b: (layer, b, 0, 0, 0))
    return pl.pallas_call(
        _attn_decode_kernel,
        grid=(DEC_BATCH,),
        in_specs=[pl.BlockSpec((None, 3, ATTN_HEADS, HEAD_DIM, 1), lambda b: (b, 0, 0, 0, 0)),
                  cache, cache,
                  pl.BlockSpec((8, CACHE_LEN), lambda b: (0, 0)),
                  pl.BlockSpec((1, CACHE_LEN), lambda b: (0, 0))],
        out_specs=pl.BlockSpec((None, ATTN_HEADS, HEAD_DIM, 1), lambda b: (b, 0, 0, 0)),
        out_shape=jax.ShapeDtypeStruct((DEC_BATCH, ATTN_HEADS, HEAD_DIM, 1), F32),
        compiler_params=_params("arbitrary"),
        name="attn_decode",
    )(qkv_cols, cache_k_t, cache_v_t, bias, mult)


ROW_CHUNKS = D_MODEL // LANES


def _store_row_tiles(ref, rows):
    n = rows.shape[0]
    for c in range(ROW_CHUNKS):
        ref[pl.ds(c, n, stride=ROW_CHUNKS), :] = rows[:, c * LANES:(c + 1) * LANES]


def _load_row_tiles(ref, n):
    return jnp.concatenate([ref[pl.ds(c, n, stride=ROW_CHUNKS), :] for c in range(ROW_CHUNKS)], axis=1)


def _post_mix_kernel(x_ref, ya_ref, yb_ref, yc_ref, yad_ref, ybd_ref, ycd_ref, wo_ref, g_ref, b_ref, rw_ref, rb_ref,
                     tri_ref, x1_ref, x1t_ref, info_ref, gate_ref, cnt_ref, wob_ref, run_ref):
    i = pl.program_id(0)

    @pl.when(i == 0)
    def _():
        wob_ref[...] = wo_ref[...].astype(BF16)
        run_ref[...] = jnp.zeros_like(run_ref)

    is_decode = i == N_TOK_TILES - 1
    mixed = jnp.concatenate([jnp.where(is_decode, d_ref[...], p_ref[...]) for p_ref, d_ref in
                             ((ya_ref, yad_ref), (yb_ref, ybd_ref), (yc_ref, ycd_ref))], axis=1)
    mix = jnp.dot(mixed.astype(BF16), wob_ref[...], preferred_element_type=F32)
    x1 = _layer_norm(DEEPNORM_ALPHA * x_ref[...] + mix, g_ref[...], b_ref[...])
    x1_ref[...] = x1
    _store_row_tiles(x1t_ref, x1)
    logits = _bdot(x1, rw_ref[...]) + rb_ref[...]
    lane = lax.broadcasted_iota(jnp.int32, (TOK_TILE, LANES), 1).astype(F32)
    work = logits
    chosen = jnp.zeros((TOK_TILE, LANES), F32)
    tops, idxs, hots = [], [], []
    for _ in range(TOP_K):
        mk = jnp.max(work, axis=-1, keepdims=True)
        idx = jnp.min(jnp.where(work == mk, lane, float(LANES)), axis=-1, keepdims=True)
        hot = lane == idx
        tops.append(mk)
        idxs.append(idx)
        hots.append(hot)
        chosen = chosen + jnp.where(hot, 1.0, 0.0)
        work = jnp.where(hot, -jnp.inf, work)
    exps = [jnp.exp(v - tops[0]) for v in tops]
    denom = exps[0] + exps[1] + exps[2] + exps[3]
    ahead = jnp.dot(tri_ref[...], chosen.astype(BF16), preferred_element_type=F32) + run_ref[0:1, :]
    ranks = [jnp.sum(jnp.where(hot, ahead, 0.0), axis=-1, keepdims=True) for hot in hots]
    info = jnp.zeros((TOK_TILE, LANES), F32)
    gates = jnp.zeros((TOK_TILE, LANES), F32)
    for k in range(TOP_K):
        info = jnp.where(lane == float(k), idxs[k], info)
        info = jnp.where(lane == float(TOP_K + k), ranks[k], info)
        gates = jnp.where(lane == float(k), exps[k] / denom, gates)
    info_ref[...] = info.astype(jnp.int32)
    gate_ref[...] = gates
    run_ref[...] = run_ref[...] + jnp.sum(chosen, axis=0, keepdims=True)
    cnt_ref[...] = run_ref[...]


def _post_mix(x, mix_prompt, mix_decode, w_o, ln_g, ln_b, router_w, router_b, tri, layer):
    row = pl.BlockSpec((TOK_TILE, D_MODEL), lambda i: (i, 0))
    vec = pl.BlockSpec((1, D_MODEL), lambda i: (0, 0))
    lanes = pl.BlockSpec((TOK_TILE, LANES), lambda i: (i, 0))
    widths = (POOL_WIDTH, ATTN_WIDTH, CONV_WIDTH)
    prompt = [pl.BlockSpec((TOK_TILE, w), lambda i: (jnp.minimum(i, N_PROMPT_TILES - 1), 0)) for w in widths]
    decode = [pl.BlockSpec((TOK_TILE, w), lambda i: (0, 0)) for w in widths]
    return pl.pallas_call(
        _post_mix_kernel,
        grid=(N_TOK_TILES,),
        in_specs=[row] + prompt + decode + [
                  pl.BlockSpec((None, D_MODEL, D_MODEL), lambda i: (layer, 0, 0), pipeline_mode=pl.Buffered(1)),
                  vec, vec,
                  pl.BlockSpec((D_MODEL, LANES), lambda i: (0, 0)),
                  pl.BlockSpec((1, LANES), lambda i: (0, 0)),
                  pl.BlockSpec((TOK_TILE, TOK_TILE), lambda i: (0, 0))],
        out_specs=[row, pl.BlockSpec((TOK_TILE * ROW_CHUNKS, LANES), lambda i: (i, 0)), lanes, lanes,
                   pl.BlockSpec((8, LANES), lambda i: (0, 0))],
        out_shape=[jax.ShapeDtypeStruct((M_TOT, D_MODEL), F32),
                   jax.ShapeDtypeStruct((M_TOT * ROW_CHUNKS, LANES), F32),
                   jax.ShapeDtypeStruct((M_TOT, LANES), jnp.int32),
                   jax.ShapeDtypeStruct((M_TOT, LANES), F32),
                   jax.ShapeDtypeStruct((8, LANES), F32)],
        scratch_shapes=[pltpu.VMEM((D_MODEL, D_MODEL), BF16), pltpu.VMEM((8, LANES), F32)],
        compiler_params=_params("arbitrary"),
        name="post_mix",
    )(x, *mix_prompt, *mix_decode, w_o, ln_g, ln_b, router_w, router_b, tri)


def _dispatch_kernel(dest_ref, fill_ref, nu_ref, x_ref, xs_hbm, zero_ref, sem):
    i = pl.program_id(0)

    @pl.when(i == 0)
    def _():
        zero_ref[...] = jnp.zeros_like(zero_ref)

        def tile_copy(start):
            return pltpu.make_async_copy(zero_ref, xs_hbm.at[pl.ds(start, EXPERT_TILE)], sem)

        def over_partial_tiles(act):
            def body(e, carry):
                @pl.when(fill_ref[e] >= 0)
                def _():
                    act(tile_copy(fill_ref[e]))
                return carry
            lax.fori_loop(0, N_EXPERTS, body, 0)

        def over_unused_tiles(act):
            def body(tile, carry):
                act(tile_copy(tile * EXPERT_TILE))
                return carry
            lax.fori_loop(nu_ref[0], N_EXPERT_TILES, body, 0)

        over_partial_tiles(lambda c: c.start())
        over_unused_tiles(lambda c: c.start())
        over_partial_tiles(lambda c: c.wait())
        over_unused_tiles(lambda c: c.wait())

    base = i * (TOK_TILE * TOP_K)
    for r in range(TOK_TILE):
        for k in range(TOP_K):
            pltpu.make_async_copy(x_ref.at[r], xs_hbm.at[dest_ref[base + r * TOP_K + k]], sem).start(priority=k % 2)
    for _ in range(TOP_K):
        pltpu.make_async_copy(x_ref, xs_hbm.at[pl.ds(0, TOK_TILE)], sem).wait()


def _dispatch(dest_flat, fill_start, n_used, x1_tiles):
    return pl.pallas_call(
        _dispatch_kernel,
        grid_spec=pltpu.PrefetchScalarGridSpec(
            num_scalar_prefetch=3,
            grid=(N_TOK_TILES,),
            in_specs=[pl.BlockSpec((TOK_TILE, ROW_CHUNKS, LANES), lambda i, d, f, n: (i, 0, 0))],
            out_specs=pl.BlockSpec(memory_space=pl.ANY),
            scratch_shapes=[pltpu.VMEM((EXPERT_TILE, ROW_CHUNKS, LANES), F32), pltpu.SemaphoreType.DMA]),
        out_shape=jax.ShapeDtypeStruct((N_SORTED, ROW_CHUNKS, LANES), F32),
        compiler_params=_params("arbitrary"),
        name="moe_dispatch",
    )(dest_flat, fill_start, n_used, x1_tiles)


def _experts_kernel(layer, te_ref, nu_ref, first_ref, next_ref, slot_ref, x_ref, w1_hbm, b1_ref, w2_hbm, b2_ref,
                    o_ref, w1f_ref, w2f_ref, w1b_ref, w2b_ref, sem):
    i = pl.program_id(0)

    def weight_copies(e, slot):
        return (pltpu.make_async_copy(w1_hbm.at[layer, e], w1f_ref.at[slot], sem.at[0, slot]),
                pltpu.make_async_copy(w2_hbm.at[layer, e], w2f_ref.at[slot], sem.at[1, slot]))

    @pl.when(i == 0)
    def _():
        for c in weight_copies(te_ref[0], 0):
            c.start()

    @pl.when(first_ref[i] == 1)
    def _():
        slot = slot_ref[i]
        for c in weight_copies(te_ref[i], slot):
            c.wait()
        w1b_ref[...] = w1f_ref[slot].astype(BF16)
        w2b_ref[...] = w2f_ref[slot].astype(BF16)

        @pl.when(next_ref[i] >= 0)
        def _():
            for c in weight_copies(next_ref[i], 1 - slot):
                c.start()

    @pl.when(i < nu_ref[0])
    def _():
        x = _load_row_tiles(x_ref, EXPERT_TILE)
        h = jnp.dot(x.astype(BF16), w1b_ref[...], preferred_element_type=F32) + b1_ref[...]
        gate = jnp.minimum(h[:, :D_EXPERT], SWIGLU_LIMIT)
        up = jnp.clip(h[:, D_EXPERT:], -SWIGLU_LIMIT, SWIGLU_LIMIT)
        act = gate * jax.nn.sigmoid(SWIGLU_ALPHA * gate) * (up + 1.0)
        _store_row_tiles(o_ref, jnp.dot(act.astype(BF16), w2b_ref[...], preferred_element_type=F32) + b2_ref[...])

    @pl.when(i >= nu_ref[0])
    def _():
        o_ref[...] = jnp.zeros_like(o_ref)


def _experts(tables, xs, w1, b1, w2, b2, layer):
    bias = lambda n: pl.BlockSpec((None, None, 1, n), lambda i, te, *_: (layer, te[i], 0, 0))
    rows = pl.BlockSpec((EXPERT_TILE * ROW_CHUNKS, LANES), lambda i, te, nu, *_: (jnp.minimum(i, nu[0] - 1), 0))
    hbm = pl.BlockSpec(memory_space=pl.ANY)
    return pl.pallas_call(
        functools.partial(_experts_kernel, layer),
        grid_spec=pltpu.PrefetchScalarGridSpec(
            num_scalar_prefetch=len(tables),
            grid=(N_EXPERT_TILES,),
            in_specs=[rows, hbm, bias(2 * D_EXPERT), hbm, bias(D_MODEL)],
            out_specs=pl.BlockSpec((EXPERT_TILE * ROW_CHUNKS, LANES), lambda i, *_: (i, 0)),
            scratch_shapes=[pltpu.VMEM((2, D_MODEL, 2 * D_EXPERT), F32), pltpu.VMEM((2, D_EXPERT, D_MODEL), F32),
                            pltpu.VMEM((D_MODEL, 2 * D_EXPERT), BF16), pltpu.VMEM((D_EXPERT, D_MODEL), BF16),
                            pltpu.SemaphoreType.DMA((2, 2))]),
        out_shape=jax.ShapeDtypeStruct((N_SORTED * ROW_CHUNKS, LANES), F32),
        compiler_params=_params("arbitrary"),
        name="moe_experts",
    )(*tables, xs, w1, b1, w2, b2)


def _expert_tables(tile_expert, n_used, counts):
    tile = jnp.arange(N_EXPERT_TILES, dtype=jnp.int32)
    opens = (tile < n_used[0]) & ((tile == 0) | (tile_expert != jnp.roll(tile_expert, 1)))
    slot = (jnp.cumsum(opens.astype(jnp.int32)) - 1) % 2
    present = jnp.where(counts[0, :N_EXPERTS] > 0, jnp.arange(N_EXPERTS, dtype=jnp.int32), N_EXPERTS)
    after = jnp.concatenate([lax.cummin(present, reverse=True)[1:], jnp.full((1,), N_EXPERTS, jnp.int32)])
    is_expert = tile_expert[:, None] == jnp.arange(N_EXPERTS, dtype=jnp.int32)
    next_expert = jnp.sum(jnp.where(is_expert, jnp.where(after < N_EXPERTS, after, -1), 0), axis=1)
    return (tile_expert, n_used, opens.astype(jnp.int32), next_expert.astype(jnp.int32), slot.astype(jnp.int32))


def _combine_rows(dest_ref, x1_ref, gate_ref, g_ref, b_ref, ys_hbm, rows_ref, sem, tile_of):
    s = pl.program_id(0)

    def gather(tile, slot):
        base = tile * (TOK_TILE * TOP_K)
        for r in range(TOK_TILE):
            for k in range(TOP_K):
                pltpu.make_async_copy(ys_hbm.at[dest_ref[base + r * TOP_K + k]],
                                      rows_ref.at[slot, k, pl.ds(r * ROW_CHUNKS, ROW_CHUNKS), :],
                                      sem.at[slot]).start(priority=k % 2)

    @pl.when(s == 0)
    def _():
        gather(tile_of(0), 0)

    for parity in range(2):
        @pl.when((s + 1 < N_TOK_TILES) & (s % 2 == parity))
        def _():
            gather(tile_of(s + 1), 1 - parity)

    slot = s % 2
    for _ in range(TOP_K):
        pltpu.make_async_copy(ys_hbm.at[pl.ds(0, TOK_TILE)], ys_hbm.at[pl.ds(0, TOK_TILE)], sem.at[slot]).wait()
    gates = gate_ref[...]
    moe = gates[:, 0:1] * _load_row_tiles(rows_ref.at[slot, 0], TOK_TILE)
    for k in range(1, TOP_K):
        moe = moe + gates[:, k:k + 1] * _load_row_tiles(rows_ref.at[slot, k], TOK_TILE)
    return _layer_norm(DEEPNORM_ALPHA * x1_ref[...] + moe, g_ref[...], b_ref[...])


def _combine_kernel(dest_ref, x1_ref, gate_ref, g_ref, b_ref, ys_hbm, o_ref, rows_ref, sem):
    o_ref[...] = _combine_rows(dest_ref, x1_ref, gate_ref, g_ref, b_ref, ys_hbm, rows_ref, sem, lambda s: s)


def _combine_in_proj_kernel(dest_ref, x1_ref, gate_ref, g_ref, b_ref, ys_hbm, w_ref, o_ref, pool_ref, conv_ref,
                            qkvd_ref, q1_ref, q4_ref, q16_ref, kvt_ref, rows_ref, sem, wb_ref, stage_ref):
    x = _combine_rows(dest_ref, x1_ref, gate_ref, g_ref, b_ref, ys_hbm, rows_ref, sem, _step_tile)
    o_ref[...] = x
    _in_proj_body(x, w_ref, wb_ref, stage_ref, pool_ref, conv_ref, qkvd_ref, q1_ref, q4_ref, q16_ref, kvt_ref)


_COMBINE_SCRATCH = [pltpu.VMEM((2, TOP_K, TOK_TILE * ROW_CHUNKS, LANES), F32), pltpu.SemaphoreType.DMA((2,))]


def _combine(dest_flat, x1, gates, ln_g, ln_b, ys_tiles):
    row = pl.BlockSpec((TOK_TILE, D_MODEL), lambda i, d: (i, 0))
    vec = pl.BlockSpec((1, D_MODEL), lambda i, d: (0, 0))
    return pl.pallas_call(
        _combine_kernel,
        grid_spec=pltpu.PrefetchScalarGridSpec(
            num_scalar_prefetch=1,
            grid=(N_TOK_TILES,),
            in_specs=[row, pl.BlockSpec((TOK_TILE, LANES), lambda i, d: (i, 0)), vec, vec,
                      pl.BlockSpec(memory_space=pl.ANY)],
            out_specs=row,
            scratch_shapes=_COMBINE_SCRATCH),
        out_shape=jax.ShapeDtypeStruct((M_TOT, D_MODEL), F32),
        compiler_params=_params("arbitrary"),
        name="moe_combine",
    )(dest_flat, x1, gates, ln_g, ln_b, ys_tiles)


def _combine_in_proj(dest_flat, x1, gates, ln_g, ln_b, ys_tiles, w_in, next_layer):
    row = pl.BlockSpec((TOK_TILE, D_MODEL), lambda s, d: (_step_tile(s), 0))
    vec = pl.BlockSpec((1, D_MODEL), lambda s, d: (0, 0))
    specs, shapes, scratch = _in_proj_outputs()
    return pl.pallas_call(
        _combine_in_proj_kernel,
        grid_spec=pltpu.PrefetchScalarGridSpec(
            num_scalar_prefetch=1,
            grid=(N_TOK_TILES,),
            in_specs=[row, pl.BlockSpec((TOK_TILE, LANES), lambda s, d: (_step_tile(s), 0)), vec, vec,
                      pl.BlockSpec(memory_space=pl.ANY), _w_in_spec(next_layer)],
            out_specs=[row] + specs,
            scratch_shapes=_COMBINE_SCRATCH + scratch),
        out_shape=[jax.ShapeDtypeStruct((M_TOT, D_MODEL), F32)] + shapes,
        compiler_params=_params("arbitrary"),
        name="moe_combine_in_proj",
    )(dest_flat, x1, gates, ln_g, ln_b, ys_tiles, w_in)


def _routing_tables(info, counts):
    top_i = info[:, :TOP_K]
    rank = info[:, TOP_K:2 * TOP_K]
    cnt = counts[0, :N_EXPERTS].astype(jnp.int32)
    padded = (cnt + EXPERT_TILE - 1) // EXPERT_TILE * EXPERT_TILE
    pad_end = jnp.cumsum(padded)
    pad_start = pad_end - padded
    hot = top_i[:, :, None] == jnp.arange(N_EXPERTS, dtype=jnp.int32)
    dest = rank + jnp.sum(jnp.where(hot, pad_start, 0), axis=-1)
    fill_start = jnp.where(cnt % EXPERT_TILE != 0, pad_end - EXPERT_TILE, -1).astype(jnp.int32)
    tile_start = jnp.arange(N_EXPERT_TILES, dtype=jnp.int32) * EXPERT_TILE
    n_used = (pad_end[-1] // EXPERT_TILE).astype(jnp.int32)
    tile_expert = jnp.sum(tile_start[:, None] >= pad_end[None, :], axis=1).astype(jnp.int32)
    last_expert = tile_expert[jnp.maximum(n_used - 1, 0)]
    tile_expert = jnp.where(tile_start < pad_end[-1], tile_expert, last_expert)
    return dest.reshape(-1).astype(jnp.int32), fill_start, tile_expert, n_used.reshape(1)


def _block_diag(pool_w):
    g, n, _ = pool_w.shape
    out = jnp.zeros((g * n, g * n), pool_w.dtype)
    for i in range(g):
        out = out.at[i * n:(i + 1) * n, i * n:(i + 1) * n].set(pool_w[i])
    return out


def kernel(x_prompt, x_sample, state_pool, state_conv, cache_attn_k, cache_attn_v, w_in, pool_w, pool_scale, conv_w, conv_b, conv_ln_g, conv_ln_b, w_o, ln1_g, ln1_b, router_w, router_b, expert_w_in, expert_b_in, expert_w_out, expert_b_out, ln2_g, ln2_b):
    bias_prompt = jnp.asarray(_attn_bias_tables())
    bias_decode, mult_decode = (jnp.asarray(t) for t in _attn_decode_tables())
    tri = jnp.asarray(np.tril(np.ones((TOK_TILE, TOK_TILE), np.float32), -1), BF16)
    x = jnp.concatenate([x_prompt.reshape(M_PROMPT, D_MODEL), x_sample.reshape(DEC_BATCH, D_MODEL),
                         jnp.zeros((M_TOT - M_PROMPT - DEC_BATCH, D_MODEL), F32)], axis=0)
    cache_k_t = cache_attn_k.transpose(0, 1, 3, 4, 2)
    cache_v_t = cache_attn_v.transpose(0, 1, 3, 4, 2)
    b_in = expert_b_in.reshape(DEPTH, N_EXPERTS, 1, 2 * D_EXPERT)
    b_out = expert_b_out.reshape(DEPTH, N_EXPERTS, 1, D_MODEL)
    outs = {name: [] for name in ("p_pool", "p_conv", "p_k", "p_v", "s_pool", "s_conv", "s_k", "s_v")}
    proj = _in_proj(x, w_in, 0)
    for l in range(DEPTH):
        u_pool, c_in, qkv_d, q1, q4, q16, kv_t = proj
        w_bd = _block_diag(pool_w[l])
        scale = pool_scale[l].reshape(1, POOL_WIDTH)
        cb, cg, cbb = (v[l].reshape(1, CONV_WIDTH) for v in (conv_b, conv_ln_g, conv_ln_b))

        ya = _pool_prompt(u_pool, w_bd, scale)
        yc, glu = _conv_prompt(c_in, conv_w[l], cb, cg, cbb)
        branches = [_attn_prompt(q.reshape(-1, QKV_WIDTH), bias_prompt[bi], d)
                    for bi, (q, (_, d)) in enumerate(zip((q1, q4, q16), DILATED))]
        yb = _attn_merge(branches)

        sp_t = state_pool[l].transpose(1, 0, 2)
        sc_t = state_conv[l].transpose(1, 0, 2)
        ya_d, yc_d, glu_d = _mix_decode(u_pool, c_in, sp_t, sc_t, w_bd, scale, conv_w[l], cb, cg, cbb)
        qkv_d = qkv_d[:DEC_BATCH]
        yb_d = _attn_decode(qkv_d.reshape(DEC_BATCH, 3, ATTN_HEADS, HEAD_DIM, 1), cache_k_t, cache_v_t,
                            bias_decode, mult_decode, l)
        yb_d = jnp.pad(yb_d.reshape(DEC_BATCH, ATTN_WIDTH), ((0, TOK_TILE - DEC_BATCH), (0, 0)))

        rw = jnp.pad(router_w[l], ((0, 0), (0, LANES - N_EXPERTS)))
        rb = jnp.pad(router_b[l], (0, LANES - N_EXPERTS), constant_values=NEG_BIG).reshape(1, LANES)
        x1, x1_tiles, info, gates, counts = _post_mix(x, (ya, yb, yc), (ya_d, yb_d, yc_d), w_o,
                                                      ln1_g[l].reshape(1, D_MODEL), ln1_b[l].reshape(1, D_MODEL),
                                                      rw, rb, tri, l)

        dest, fill_start, tile_expert, n_used = _routing_tables(info, counts)
        xs = _dispatch(dest, fill_start, n_used, x1_tiles.reshape(M_TOT, ROW_CHUNKS, LANES))
        ys = _experts(_expert_tables(tile_expert, n_used, counts), xs.reshape(N_SORTED * ROW_CHUNKS, LANES),
                      expert_w_in, b_in, expert_w_out, b_out, l)
        combine_args = (dest, x1, gates, ln2_g[l].reshape(1, D_MODEL), ln2_b[l].reshape(1, D_MODEL),
                        ys.reshape(N_SORTED, ROW_CHUNKS, LANES))
        if l + 1 < DEPTH:
            x, *proj = _combine_in_proj(*combine_args, w_in, l + 1)
        else:
            x = _combine(*combine_args)

        kv_t = kv_t[:BATCH].reshape(BATCH, 2, ATTN_HEADS, HEAD_DIM, CACHE_LEN).transpose(1, 0, 4, 2, 3)
        last_rows = lambda a, n: a[:M_PROMPT].reshape(BATCH, SEQ, a.shape[-1])[:, SEQ - n:]
        outs["p_pool"].append(last_rows(u_pool, POOL_BUF))
        outs["p_conv"].append(last_rows(glu, CONV_BUF))
        outs["p_k"].append(kv_t[0])
        outs["p_v"].append(kv_t[1])
        u_d = u_pool[M_PROMPT:M_PROMPT + DEC_BATCH]
        outs["s_pool"].append(jnp.concatenate([state_pool[l][:, 1:], u_d[:, None]], axis=1))
        outs["s_conv"].append(jnp.concatenate([state_conv[l][:, 1:], glu_d[:, None]], axis=1))
        outs["s_k"].append(qkv_d[:, ATTN_WIDTH:2 * ATTN_WIDTH].reshape(DEC_BATCH, 1, ATTN_HEADS, HEAD_DIM))
        outs["s_v"].append(qkv_d[:, 2 * ATTN_WIDTH:].reshape(DEC_BATCH, 1, ATTN_HEADS, HEAD_DIM))

    y_p = x[:M_PROMPT].reshape(BATCH, SEQ, D_MODEL)
    y_s = x[M_PROMPT:M_PROMPT + DEC_BATCH].reshape(DEC_BATCH, 1, D_MODEL)
    return (y_p, y_s) + tuple(jnp.stack(outs[name]) for name in
                              ("p_pool", "p_conv", "p_k", "p_v", "s_pool", "s_conv", "s_k", "s_v"))
```

```python
import functools
import math

import numpy as np
import jax
import jax.numpy as jnp
from jax import lax
from jax.experimental import pallas as pl
from jax.experimental.pallas import tpu as pltpu

F32 = jnp.float32
BF16 = jnp.bfloat16

D_MODEL = 1024
BATCH = 4
SEQ = 4096
DEPTH = 4
DEC_BATCH = 32
PAST_LEN = 8192
POOL_WIDTH = 256
POOL_WINDOWS = (2, 4, 8, 16)
POOL_BUF = 15
ATTN_HEADS = 6
HEAD_DIM = 64
ATTN_WIDTH = ATTN_HEADS * HEAD_DIM
DILATED = ((128, 1), (512, 4), (2048, 16))
ATTN_BLOCK = 128
CACHE_LEN = 2048
CONV_WIDTH = 384
CONV_KERNEL = 31
CONV_BUF = 30
N_EXPERTS = 32
TOP_K = 4
D_EXPERT = 1024
SWIGLU_ALPHA = 1.702
SWIGLU_LIMIT = 7.0
DEEPNORM_ALPHA = (2 * DEPTH) ** 0.25
LN_EPS = 1e-5
NEG_BIG = -1e30
LANES = 128

M_PROMPT = BATCH * SEQ
TOK_TILE = 256
M_TOT = M_PROMPT + TOK_TILE
N_TOK_TILES = M_TOT // TOK_TILE
SEQ_TILE = 512
ATTN_Q_TILE = 2048
EXPERT_TILE = 512
N_ASSIGN = M_TOT * TOP_K
N_EXPERT_TILES = -(-N_ASSIGN // EXPERT_TILE) + N_EXPERTS
N_SORTED = N_EXPERT_TILES * EXPERT_TILE
VMEM_LIMIT = 56 * 1024 * 1024


def _alibi_slopes(n):
    def pow2(k):
        start = 2.0 ** (-8.0 / k)
        return [start ** (i + 1) for i in range(k)]
    if math.log2(n).is_integer():
        return pow2(n)
    c = 2 ** math.floor(math.log2(n))
    return pow2(c) + pow2(2 * c)[0::2][: n - c]


def _bdot(a, b):
    return jnp.dot(a.astype(BF16), b.astype(BF16), preferred_element_type=F32)


def _layer_norm(x, g, b):
    mu = jnp.mean(x, axis=-1, keepdims=True)
    xc = x - mu
    var = jnp.mean(xc * xc, axis=-1, keepdims=True)
    return xc * lax.rsqrt(var + LN_EPS) * g + b


def _params(*sem):
    return pltpu.CompilerParams(dimension_semantics=sem, vmem_limit_bytes=VMEM_LIMIT)


N_PROMPT_TILES = M_PROMPT // TOK_TILE
TILES_PER_SEQ = SEQ // TOK_TILE
QKV_WIDTH = 3 * ATTN_WIDTH
QKV_CHUNKS = QKV_WIDTH // LANES


def _cast_w_in(w_ref, wb_ref):
    @pl.when(pl.program_id(0) == 0)
    def _():
        wb_ref[...] = w_ref[...].astype(BF16)


def _in_proj_body(x, wb_ref, stage_ref, pool_ref, conv_ref, qkvd_ref, q1_ref, q4_ref, q16_ref, kvt_ref):
    res = jnp.dot(x.astype(BF16), wb_ref[...], preferred_element_type=F32)
    qkv = res[:, POOL_WIDTH:POOL_WIDTH + QKV_WIDTH]
    pool_ref[...] = res[:, :POOL_WIDTH]
    conv_ref[...] = res[:, POOL_WIDTH + QKV_WIDTH:]
    qkvd_ref[...] = qkv
    kvt_ref[...] = jnp.transpose(qkv[:, ATTN_WIDTH:])
    q1_ref[...] = qkv.astype(BF16)
    for c in range(QKV_CHUNKS):
        stage_ref[c] = qkv[:, c * LANES:(c + 1) * LANES]
    for ref, d in ((q4_ref, DILATED[1][1]), (q16_ref, DILATED[2][1])):
        n = TOK_TILE // d
        for r in range(d):
            rows = [stage_ref[c, pl.ds(r, n, stride=d), :] for c in range(QKV_CHUNKS)]
            ref[r] = jnp.concatenate(rows, axis=1).astype(BF16)


def _in_proj_kernel(x_ref, w_ref, pool_ref, conv_ref, qkvd_ref, q1_ref, q4_ref, q16_ref, kvt_ref, wb_ref, stage_ref):
    _cast_w_in(w_ref, wb_ref)
    _in_proj_body(x_ref[...], wb_ref, stage_ref, pool_ref, conv_ref, qkvd_ref, q1_ref, q4_ref, q16_ref, kvt_ref)


CACHE_TILE0 = (SEQ - CACHE_LEN) // TOK_TILE
N_IN = POOL_WIDTH + QKV_WIDTH + 2 * CONV_WIDTH


def _step_tile(s):
    return (s + N_TOK_TILES - 1) % N_TOK_TILES


def _in_proj_outputs():
    d4, d16 = DILATED[1][1], DILATED[2][1]
    ptile = lambda s: jnp.maximum(s - 1, 0)
    by_residue = lambda d: pl.BlockSpec(
        (None, d, TOK_TILE // d, QKV_WIDTH),
        lambda s, *_: (ptile(s) // TILES_PER_SEQ, 0, ptile(s) % TILES_PER_SEQ, 0))
    specs = [pl.BlockSpec((TOK_TILE, POOL_WIDTH), lambda s, *_: (_step_tile(s), 0)),
             pl.BlockSpec((TOK_TILE, 2 * CONV_WIDTH), lambda s, *_: (_step_tile(s), 0)),
             pl.BlockSpec((TOK_TILE, QKV_WIDTH), lambda s, *_: (jnp.minimum(s, 1), 0)),
             pl.BlockSpec((TOK_TILE, QKV_WIDTH), lambda s, *_: (_step_tile(s), 0)),
             by_residue(d4), by_residue(d16),
             pl.BlockSpec((None, 2 * ATTN_WIDTH, TOK_TILE),
                          lambda s, *_: (ptile(s) // TILES_PER_SEQ, 0,
                                         jnp.maximum(ptile(s) % TILES_PER_SEQ - CACHE_TILE0, 0)))]
    shapes = [jax.ShapeDtypeStruct((M_TOT, POOL_WIDTH), F32),
              jax.ShapeDtypeStruct((M_TOT, 2 * CONV_WIDTH), F32),
              jax.ShapeDtypeStruct((2 * TOK_TILE, QKV_WIDTH), F32),
              jax.ShapeDtypeStruct((M_TOT, QKV_WIDTH), BF16),
              jax.ShapeDtypeStruct((BATCH, d4, SEQ // d4, QKV_WIDTH), BF16),
              jax.ShapeDtypeStruct((BATCH, d16, SEQ // d16, QKV_WIDTH), BF16),
              jax.ShapeDtypeStruct((BATCH, 2 * ATTN_WIDTH, CACHE_LEN), F32)]
    scratch = [pltpu.VMEM((D_MODEL, N_IN), BF16), pltpu.VMEM((QKV_CHUNKS, TOK_TILE, LANES), F32)]
    return specs, shapes, scratch


def _w_in_spec(layer):
    return pl.BlockSpec((None, D_MODEL, N_IN), lambda s, *_: (layer, 0, 0), pipeline_mode=pl.Buffered(1))


def _in_proj(x, w_in, layer):
    specs, shapes, scratch = _in_proj_outputs()
    return pl.pallas_call(
        _in_proj_kernel,
        grid=(N_TOK_TILES,),
        in_specs=[pl.BlockSpec((TOK_TILE, D_MODEL), lambda s: (_step_tile(s), 0)), _w_in_spec(layer)],
        out_specs=specs,
        out_shape=shapes,
        scratch_shapes=scratch,
        compiler_params=_params("arbitrary"),
        name="in_proj",
    )(x, w_in)


def _pool_select(lane, vals):
    out = vals[-1]
    for g in range(len(vals) - 2, -1, -1):
        out = jnp.where(lane < (g + 1) * 64, vals[g], out)
    return out


def _pool_prompt_kernel(u_ref, w_ref, scale_ref, o_ref, halo_ref):
    t = pl.program_id(1)

    @pl.when(t == 0)
    def _():
        halo_ref[...] = jnp.zeros_like(halo_ref)

    u = u_ref[...]
    n = SEQ_TILE + 16
    ext = jnp.concatenate([halo_ref[...], u], axis=0)
    p2 = ext + pltpu.roll(ext, 1, 0)
    p4 = p2 + pltpu.roll(p2, 2, 0)
    p8 = p4 + pltpu.roll(p4, 4, 0)
    p16 = p8 + pltpu.roll(p8, 8, 0)
    lane = lax.broadcasted_iota(jnp.int32, (SEQ_TILE, POOL_WIDTH), 1)
    row = lax.broadcasted_iota(jnp.int32, (SEQ_TILE, POOL_WIDTH), 0)
    pos1 = (row + t * SEQ_TILE + 1).astype(F32)
    wsel = _pool_select(lane, [jnp.full((SEQ_TILE, POOL_WIDTH), float(w), F32) for w in POOL_WINDOWS])
    cnt = jnp.minimum(wsel, pos1)
    sums = _pool_select(lane, [p[16:n] for p in (p2, p4, p8, p16)])
    d = sums / cnt - u
    o_ref[...] = _bdot(d, w_ref[...]) * scale_ref[...]
    halo_ref[...] = u[SEQ_TILE - 16:]


def _pool_prompt(u_pool, w_bd, scale):
    nt = SEQ // SEQ_TILE
    return pl.pallas_call(
        _pool_prompt_kernel,
        grid=(BATCH, nt),
        in_specs=[pl.BlockSpec((SEQ_TILE, POOL_WIDTH), lambda b, t: (b * nt + t, 0)),
                  pl.BlockSpec((POOL_WIDTH, POOL_WIDTH), lambda b, t: (0, 0)),
                  pl.BlockSpec((1, POOL_WIDTH), lambda b, t: (0, 0))],
        out_specs=pl.BlockSpec((SEQ_TILE, POOL_WIDTH), lambda b, t: (b * nt + t, 0)),
        out_shape=jax.ShapeDtypeStruct((M_PROMPT, POOL_WIDTH), F32),
        scratch_shapes=[pltpu.VMEM((16, POOL_WIDTH), F32)],
        compiler_params=_params("arbitrary", "arbitrary"),
        name="pool_prompt",
    )(u_pool, w_bd, scale)


CONV_HALO = 32


def _conv_prompt_kernel(c_ref, w_ref, b_ref, g_ref, bb_ref, y_ref, u_ref, ext_ref, sh_ref):
    t = pl.program_id(1)

    @pl.when(t == 0)
    def _():
        ext_ref[0:CONV_HALO, :] = jnp.zeros((CONV_HALO, CONV_WIDTH), F32)

    c = c_ref[...]
    u = c[:, :CONV_WIDTH] * jax.nn.sigmoid(c[:, CONV_WIDTH:])
    u_ref[...] = u
    ext_ref[CONV_HALO:CONV_HALO + SEQ_TILE, :] = u
    span = SEQ_TILE + CONV_HALO - 8
    for ph in range(1, 8):
        sh_ref[ph, 0:span, :] = ext_ref[pl.ds(ph, span), :]

    acc = jnp.zeros((SEQ_TILE, CONV_WIDTH), F32)
    for j in range(CONV_KERNEL):
        off = j + 2
        a, ph = off // 8, off % 8
        tap = ext_ref[8 * a:8 * a + SEQ_TILE, :] if ph == 0 else sh_ref[ph, 8 * a:8 * a + SEQ_TILE, :]
        acc = acc + tap * w_ref[j:j + 1, :]
    y = _layer_norm(acc + b_ref[...], g_ref[...], bb_ref[...])
    y_ref[...] = y * jax.nn.sigmoid(y)
    ext_ref[0:CONV_HALO, :] = u[SEQ_TILE - CONV_HALO:]


def _conv_prompt(c_in, conv_w, conv_b, ln_g, ln_b):
    nt = SEQ // SEQ_TILE
    vec = pl.BlockSpec((1, CONV_WIDTH), lambda b, t: (0, 0))
    return pl.pallas_call(
        _conv_prompt_kernel,
        grid=(BATCH, nt),
        in_specs=[pl.BlockSpec((SEQ_TILE, 2 * CONV_WIDTH), lambda b, t: (b * nt + t, 0)),
                  pl.BlockSpec((CONV_KERNEL, CONV_WIDTH), lambda b, t: (0, 0)),
                  vec, vec, vec],
        out_specs=[pl.BlockSpec((SEQ_TILE, CONV_WIDTH), lambda b, t: (b * nt + t, 0)),
                   pl.BlockSpec((SEQ_TILE, CONV_WIDTH), lambda b, t: (b * nt + t, 0))],
        out_shape=[jax.ShapeDtypeStruct((M_PROMPT, CONV_WIDTH), F32),
                   jax.ShapeDtypeStruct((M_PROMPT, CONV_WIDTH), F32)],
        scratch_shapes=[pltpu.VMEM((SEQ_TILE + CONV_HALO, CONV_WIDTH), F32),
                        pltpu.VMEM((8, SEQ_TILE + CONV_HALO, CONV_WIDTH), F32)],
        compiler_params=_params("arbitrary", "arbitrary"),
        name="conv_prompt",
    )(c_in, conv_w, conv_b, ln_g, ln_b)


def _attn_bias_tables():
    slopes = _alibi_slopes(ATTN_HEADS)
    qi = np.arange(ATTN_BLOCK)[:, None] + ATTN_BLOCK
    ki = np.arange(2 * ATTN_BLOCK)[None, :]
    delta = qi - ki
    tabs = np.zeros((len(DILATED), ATTN_HEADS // 2, 4, ATTN_BLOCK, 2 * ATTN_BLOCK), np.float32)
    for bi, (w, d) in enumerate(DILATED):
        n_back = w // d
        band = (delta >= 0) & (delta <= n_back)
        for h in range(ATTN_HEADS):
            bias = -slopes[h] * (delta * d).astype(np.float32)
            for first in range(2):
                valid = band & (ki >= ATTN_BLOCK) if first else band
                tabs[bi, h // 2, 2 * (h % 2) + first] = np.where(valid, bias, NEG_BIG)
    return tabs


def _attn_prompt_kernel(blocks_per_seq, q_ref, kp_ref, kc_ref, vp_ref, vc_ref, bias_ref, o_ref, lse_ref):
    t = pl.program_id(1)
    kk = jnp.concatenate([kp_ref[...], kc_ref[...]], axis=0)
    vv = jnp.concatenate([vp_ref[...], vc_ref[...]], axis=0)
    lane_q = lax.broadcasted_iota(jnp.int32, (ATTN_BLOCK, 2 * HEAD_DIM), 1)
    for j in range(ATTN_Q_TILE // ATTN_BLOCK):
        blk = t * (ATTN_Q_TILE // ATTN_BLOCK) + j
        first = (blk % blocks_per_seq == 0).astype(jnp.int32)
        q = q_ref[j * ATTN_BLOCK:(j + 1) * ATTN_BLOCK, :]
        keys = kk[j * ATTN_BLOCK:(j + 2) * ATTN_BLOCK]
        vals = vv[j * ATTN_BLOCK:(j + 2) * ATTN_BLOCK]
        outs, lses = [], []
        for h in range(2):
            in_head = (lane_q >= h * HEAD_DIM) & (lane_q < (h + 1) * HEAD_DIM)
            qh = jnp.where(in_head, q, jnp.zeros_like(q))
            s = lax.dot_general(qh, keys, (((1,), (1,)), ((), ())), preferred_element_type=F32)
            s = s * (HEAD_DIM ** -0.5) + bias_ref[2 * h + first]
            m = jnp.max(s, axis=-1, keepdims=True)
            p = jnp.exp(s - m)
            l = jnp.sum(p, axis=-1, keepdims=True)
            acc = jnp.dot(p.astype(BF16), vals, preferred_element_type=F32)
            outs.append(acc / l)
            lses.append(m + jnp.log(l))
        o_ref[j * ATTN_BLOCK:(j + 1) * ATTN_BLOCK, :] = jnp.where(lane_q < HEAD_DIM, outs[0], outs[1])
        lse_ref[j * ATTN_BLOCK:(j + 1) * ATTN_BLOCK, :] = jnp.where(lane_q < HEAD_DIM, lses[0], lses[1])


def _attn_prompt(qkv, bias_tab, dilation):
    n_hp = ATTN_HEADS // 2
    per_tile = ATTN_Q_TILE // ATTN_BLOCK
    nt = M_PROMPT // ATTN_Q_TILE
    width = 2 * HEAD_DIM
    cur = lambda off: pl.BlockSpec((ATTN_Q_TILE, width), lambda hp, t: (t, off * n_hp + hp))
    prev = lambda off: pl.BlockSpec((ATTN_BLOCK, width),
                                    lambda hp, t: (jnp.maximum(t * per_tile - 1, 0), off * n_hp + hp))
    out = pl.BlockSpec((ATTN_Q_TILE, width), lambda hp, t: (t, hp))
    return pl.pallas_call(
        functools.partial(_attn_prompt_kernel, SEQ // dilation // ATTN_BLOCK),
        grid=(n_hp, nt),
        in_specs=[cur(0), prev(1), cur(1), prev(2), cur(2),
                  pl.BlockSpec((None, 4, ATTN_BLOCK, 2 * ATTN_BLOCK), lambda hp, t: (hp, 0, 0, 0))],
        out_specs=[out, out],
        out_shape=[jax.ShapeDtypeStruct((M_PROMPT, ATTN_WIDTH), F32),
                   jax.ShapeDtypeStruct((M_PROMPT, ATTN_WIDTH), F32)],
        compiler_params=_params("arbitrary", "arbitrary"),
        name=f"attn_prompt_d{dilation}",
    )(qkv, qkv, qkv, qkv, qkv, bias_tab)


ATTN_CHUNKS = ATTN_WIDTH // LANES


def _attn_merge_kernel(o1_ref, l1_ref, o4_ref, l4_ref, o16_ref, l16_ref, y_ref, so_ref, sl_ref):
    for j, (o_ref, l_ref, d) in enumerate(((o4_ref, l4_ref, DILATED[1][1]), (o16_ref, l16_ref, DILATED[2][1]))):
        n = TOK_TILE // d
        for r in range(d):
            ov, lv = o_ref[r], l_ref[r]
            for c in range(ATTN_CHUNKS):
                so_ref[j, c, pl.ds(r, n, stride=d), :] = ov[:, c * LANES:(c + 1) * LANES]
                sl_ref[j, c, pl.ds(r, n, stride=d), :] = lv[:, c * LANES:(c + 1) * LANES]
    outs = [o1_ref[...]] + [jnp.concatenate([so_ref[j, c] for c in range(ATTN_CHUNKS)], axis=1) for j in range(2)]
    lses = [l1_ref[...]] + [jnp.concatenate([sl_ref[j, c] for c in range(ATTN_CHUNKS)], axis=1) for j in range(2)]
    top = jnp.maximum(jnp.maximum(lses[0], lses[1]), lses[2])
    ws = [jnp.exp(v - top) for v in lses]
    y_ref[...] = (ws[0] * outs[0] + ws[1] * outs[1] + ws[2] * outs[2]) / (ws[0] + ws[1] + ws[2])


def _attn_merge(branches):
    (o1, l1), (o4, l4), (o16, l16) = branches
    d4, d16 = DILATED[1][1], DILATED[2][1]
    nat = pl.BlockSpec((TOK_TILE, ATTN_WIDTH), lambda i: (i, 0))
    by_residue = lambda d: pl.BlockSpec((None, d, TOK_TILE // d, ATTN_WIDTH),
                                        lambda i: (i // TILES_PER_SEQ, 0, i % TILES_PER_SEQ, 0))
    view = lambda a, d: a.reshape(BATCH, d, SEQ // d, ATTN_WIDTH)
    return pl.pallas_call(
        _attn_merge_kernel,
        grid=(N_PROMPT_TILES,),
        in_specs=[nat, nat, by_residue(d4), by_residue(d4), by_residue(d16), by_residue(d16)],
        out_specs=nat,
        out_shape=jax.ShapeDtypeStruct((M_PROMPT, ATTN_WIDTH), F32),
        scratch_shapes=[pltpu.VMEM((2, ATTN_CHUNKS, TOK_TILE, LANES), F32),
                        pltpu.VMEM((2, ATTN_CHUNKS, TOK_TILE, LANES), F32)],
        compiler_params=_params("arbitrary"),
        name="attn_merge",
    )(o1, l1, view(o4, d4), view(l4, d4), view(o16, d16), view(l16, d16))


def _mix_decode_kernel(u_ref, c_ref, sp_ref, sc_ref, pw_ref, ps_ref, cw_ref, cb_ref, cg_ref, cbb_ref,
                       ya_ref, yc_ref, glu_ref):
    nb = DEC_BATCH
    u = u_ref[0:nb, :]
    lane = lax.broadcasted_iota(jnp.int32, (nb, POOL_WIDTH), 1)
    run = u
    sums = []
    for back in range(1, max(POOL_WINDOWS)):
        run = run + sp_ref[POOL_BUF - back]
        if back + 1 in POOL_WINDOWS:
            sums.append(run)
    wsel = _pool_select(lane, [jnp.full((nb, POOL_WIDTH), float(w), F32) for w in POOL_WINDOWS])
    d = _pool_select(lane, sums) / wsel - u
    ya_ref[...] = jnp.zeros_like(ya_ref)
    ya_ref[0:nb, :] = _bdot(d, pw_ref[...]) * ps_ref[...]

    c = c_ref[0:nb, :]
    glu = c[:, :CONV_WIDTH] * jax.nn.sigmoid(c[:, CONV_WIDTH:])
    glu_ref[...] = glu
    acc = glu * cw_ref[CONV_KERNEL - 1:CONV_KERNEL, :]
    for j in range(CONV_BUF):
        acc = acc + sc_ref[j] * cw_ref[j:j + 1, :]
    y = _layer_norm(acc + cb_ref[...], cg_ref[...], cbb_ref[...])
    yc_ref[...] = jnp.zeros_like(yc_ref)
    yc_ref[0:nb, :] = y * jax.nn.sigmoid(y)


def _mix_decode(u_pool, c_in, sp_t, sc_t, w_bd, scale, conv_w, conv_b, ln_g, ln_b):
    last = N_TOK_TILES - 1
    full = lambda shape: pl.BlockSpec(shape, lambda i: (0,) * len(shape))
    return pl.pallas_call(
        _mix_decode_kernel,
        grid=(1,),
        in_specs=[pl.BlockSpec((TOK_TILE, POOL_WIDTH), lambda i: (last, 0)),
                  pl.BlockSpec((TOK_TILE, 2 * CONV_WIDTH), lambda i: (last, 0)),
                  full((POOL_BUF, DEC_BATCH, POOL_WIDTH)), full((CONV_BUF, DEC_BATCH, CONV_WIDTH)),
                  full((POOL_WIDTH, POOL_WIDTH)), full((1, POOL_WIDTH)),
                  full((CONV_KERNEL, CONV_WIDTH)), full((1, CONV_WIDTH)), full((1, CONV_WIDTH)),
                  full((1, CONV_WIDTH))],
        out_specs=[full((TOK_TILE, POOL_WIDTH)), full((TOK_TILE, CONV_WIDTH)), full((DEC_BATCH, CONV_WIDTH))],
        out_shape=[jax.ShapeDtypeStruct((TOK_TILE, POOL_WIDTH), F32),
                   jax.ShapeDtypeStruct((TOK_TILE, CONV_WIDTH), F32),
                   jax.ShapeDtypeStruct((DEC_BATCH, CONV_WIDTH), F32)],
        compiler_params=_params("arbitrary"),
        name="mix_decode",
    )(u_pool, c_in, sp_t, sc_t, w_bd, scale, conv_w, conv_b, ln_g, ln_b)


def _attn_decode_tables():
    slopes = _alibi_slopes(ATTN_HEADS)
    pos = np.arange(CACHE_LEN)
    dist = (CACHE_LEN - pos).astype(np.float32)
    bias = np.zeros((8, CACHE_LEN), np.float32)
    for h in range(ATTN_HEADS):
        bias[h] = -slopes[h] * dist
    mult = np.zeros((1, CACHE_LEN), np.float32)
    for w, d in DILATED:
        mult[0] += ((CACHE_LEN - pos) % d == 0) & (CACHE_LEN - pos <= w)
    return bias, mult


def _round_bf16(x):
    return x.astype(BF16).astype(F32)


def _attn_decode_kernel(qkv_ref, k_ref, v_ref, bias_ref, mult_ref, y_ref):
    scale = HEAD_DIM ** -0.5
    mult = mult_ref[...]
    n_branch = float(len(DILATED))
    for h in range(ATTN_HEADS):
        q = _round_bf16(qkv_ref[0, h])
        k_new = _round_bf16(qkv_ref[1, h])
        v_new = _round_bf16(qkv_ref[2, h])
        s = jnp.sum(_round_bf16(k_ref[h]) * q, axis=0, keepdims=True) * scale + bias_ref[h:h + 1, :]
        s = jnp.where(mult > 0.0, s, NEG_BIG)
        s_self = jnp.sum(q * k_new, axis=0, keepdims=True) * scale
        m = jnp.maximum(jnp.max(s, axis=1, keepdims=True), s_self)
        p = mult * jnp.exp(s - m)
        p_self = n_branch * jnp.exp(s_self - m)
        l = jnp.sum(p, axis=1, keepdims=True) + p_self
        acc = jnp.sum(_round_bf16(v_ref[h]) * _round_bf16(p), axis=1, keepdims=True) + _round_bf16(p_self) * v_new
        y_ref[h] = acc / l


def _attn_decode(qkv_cols, cache_k_t, cache_v_t, bias, mult, layer):
    cache = pl.BlockSpec((None, None, ATTN_HEADS, HEAD_DIM, CACHE_LEN), lambda b: (layer, b, 0, 0, 0))
    return pl.pallas_call(
        _attn_decode_kernel,
        grid=(DEC_BATCH,),
        in_specs=[pl.BlockSpec((None, 3, ATTN_HEADS, HEAD_DIM, 1), lambda b: (b, 0, 0, 0, 0)),
                  cache, cache,
                  pl.BlockSpec((8, CACHE_LEN), lambda b: (0, 0)),
                  pl.BlockSpec((1, CACHE_LEN), lambda b: (0, 0))],
        out_specs=pl.BlockSpec((None, ATTN_HEADS, HEAD_DIM, 1), lambda b: (b, 0, 0, 0)),
        out_shape=jax.ShapeDtypeStruct((DEC_BATCH, ATTN_HEADS, HEAD_DIM, 1), F32),
        compiler_params=_params("arbitrary"),
        name="attn_decode",
    )(qkv_cols, cache_k_t, cache_v_t, bias, mult)


ROW_CHUNKS = D_MODEL // LANES


def _store_row_tiles(ref, rows):
    n = rows.shape[0]
    for c in range(ROW_CHUNKS):
        ref[pl.ds(c, n, stride=ROW_CHUNKS), :] = rows[:, c * LANES:(c + 1) * LANES]


def _load_row_tiles(ref, n):
    return jnp.concatenate([ref[pl.ds(c, n, stride=ROW_CHUNKS), :] for c in range(ROW_CHUNKS)], axis=1)


def _post_mix_kernel(x_ref, ya_ref, yb_ref, yc_ref, yad_ref, ybd_ref, ycd_ref, wo_ref, g_ref, b_ref, rw_ref, rb_ref,
                     tri_ref, x1_ref, x1t_ref, info_ref, gate_ref, cnt_ref, wob_ref, run_ref):
    i = pl.program_id(0)

    @pl.when(i == 0)
    def _():
        wob_ref[...] = wo_ref[...].astype(BF16)
        run_ref[...] = jnp.zeros_like(run_ref)

    is_decode = i == N_TOK_TILES - 1
    mixed = jnp.concatenate([jnp.where(is_decode, d_ref[...], p_ref[...]) for p_ref, d_ref in
                             ((ya_ref, yad_ref), (yb_ref, ybd_ref), (yc_ref, ycd_ref))], axis=1)
    mix = jnp.dot(mixed.astype(BF16), wob_ref[...], preferred_element_type=F32)
    x1 = _layer_norm(DEEPNORM_ALPHA * x_ref[...] + mix, g_ref[...], b_ref[...])
    x1_ref[...] = x1
    _store_row_tiles(x1t_ref, x1)
    logits = _bdot(x1, rw_ref[...]) + rb_ref[...]
    lane = lax.broadcasted_iota(jnp.int32, (TOK_TILE, LANES), 1).astype(F32)
    work = logits
    chosen = jnp.zeros((TOK_TILE, LANES), F32)
    tops, idxs, hots = [], [], []
    for _ in range(TOP_K):
        mk = jnp.max(work, axis=-1, keepdims=True)
        idx = jnp.min(jnp.where(work == mk, lane, float(LANES)), axis=-1, keepdims=True)
        hot = lane == idx
        tops.append(mk)
        idxs.append(idx)
        hots.append(hot)
        chosen = chosen + jnp.where(hot, 1.0, 0.0)
        work = jnp.where(hot, -jnp.inf, work)
    exps = [jnp.exp(v - tops[0]) for v in tops]
    denom = exps[0] + exps[1] + exps[2] + exps[3]
    ahead = jnp.dot(tri_ref[...], chosen.astype(BF16), preferred_element_type=F32) + run_ref[0:1, :]
    ranks = [jnp.sum(jnp.where(hot, ahead, 0.0), axis=-1, keepdims=True) for hot in hots]
    info = jnp.zeros((TOK_TILE, LANES), F32)
    gates = jnp.zeros((TOK_TILE, LANES), F32)
    for k in range(TOP_K):
        info = jnp.where(lane == float(k), idxs[k], info)
        info = jnp.where(lane == float(TOP_K + k), ranks[k], info)
        gates = jnp.where(lane == float(k), exps[k] / denom, gates)
    info_ref[...] = info.astype(jnp.int32)
    gate_ref[...] = gates
    run_ref[...] = run_ref[...] + jnp.sum(chosen, axis=0, keepdims=True)
    cnt_ref[...] = run_ref[...]


def _post_mix(x, mix_prompt, mix_decode, w_o, ln_g, ln_b, router_w, router_b, tri, layer):
    row = pl.BlockSpec((TOK_TILE, D_MODEL), lambda i: (i, 0))
    vec = pl.BlockSpec((1, D_MODEL), lambda i: (0, 0))
    lanes = pl.BlockSpec((TOK_TILE, LANES), lambda i: (i, 0))
    widths = (POOL_WIDTH, ATTN_WIDTH, CONV_WIDTH)
    prompt = [pl.BlockSpec((TOK_TILE, w), lambda i: (jnp.minimum(i, N_PROMPT_TILES - 1), 0)) for w in widths]
    decode = [pl.BlockSpec((TOK_TILE, w), lambda i: (0, 0)) for w in widths]
    return pl.pallas_call(
        _post_mix_kernel,
        grid=(N_TOK_TILES,),
        in_specs=[row] + prompt + decode + [
                  pl.BlockSpec((None, D_MODEL, D_MODEL), lambda i: (layer, 0, 0), pipeline_mode=pl.Buffered(1)),
                  vec, vec,
                  pl.BlockSpec((D_MODEL, LANES), lambda i: (0, 0)),
                  pl.BlockSpec((1, LANES), lambda i: (0, 0)),
                  pl.BlockSpec((TOK_TILE, TOK_TILE), lambda i: (0, 0))],
        out_specs=[row, pl.BlockSpec((TOK_TILE * ROW_CHUNKS, LANES), lambda i: (i, 0)), lanes, lanes,
                   pl.BlockSpec((8, LANES), lambda i: (0, 0))],
        out_shape=[jax.ShapeDtypeStruct((M_TOT, D_MODEL), F32),
                   jax.ShapeDtypeStruct((M_TOT * ROW_CHUNKS, LANES), F32),
                   jax.ShapeDtypeStruct((M_TOT, LANES), jnp.int32),
                   jax.ShapeDtypeStruct((M_TOT, LANES), F32),
                   jax.ShapeDtypeStruct((8, LANES), F32)],
        scratch_shapes=[pltpu.VMEM((D_MODEL, D_MODEL), BF16), pltpu.VMEM((8, LANES), F32)],
        compiler_params=_params("arbitrary"),
        name="post_mix",
    )(x, *mix_prompt, *mix_decode, w_o, ln_g, ln_b, router_w, router_b, tri)


def _dispatch_kernel(dest_ref, fill_ref, nu_ref, x_ref, xs_hbm, zero_ref, sem):
    i = pl.program_id(0)

    @pl.when(i == 0)
    def _():
        zero_ref[...] = jnp.zeros_like(zero_ref)

        def tile_copy(start):
            return pltpu.make_async_copy(zero_ref, xs_hbm.at[pl.ds(start, EXPERT_TILE)], sem)

        def over_partial_tiles(act):
            def body(e, carry):
                @pl.when(fill_ref[e] >= 0)
                def _():
                    act(tile_copy(fill_ref[e]))
                return carry
            lax.fori_loop(0, N_EXPERTS, body, 0)

        def over_unused_tiles(act):
            def body(tile, carry):
                act(tile_copy(tile * EXPERT_TILE))
                return carry
            lax.fori_loop(nu_ref[0], N_EXPERT_TILES, body, 0)

        over_partial_tiles(lambda c: c.start())
        over_unused_tiles(lambda c: c.start())
        over_partial_tiles(lambda c: c.wait())
        over_unused_tiles(lambda c: c.wait())

    base = i * (TOK_TILE * TOP_K)
    for r in range(TOK_TILE):
        for k in range(TOP_K):
            pltpu.make_async_copy(x_ref.at[r], xs_hbm.at[dest_ref[base + r * TOP_K + k]], sem).start(priority=k % 2)
    for _ in range(TOP_K):
        pltpu.make_async_copy(x_ref, xs_hbm.at[pl.ds(0, TOK_TILE)], sem).wait()


def _dispatch(dest_flat, fill_start, n_used, x1_tiles):
    return pl.pallas_call(
        _dispatch_kernel,
        grid_spec=pltpu.PrefetchScalarGridSpec(
            num_scalar_prefetch=3,
            grid=(N_TOK_TILES,),
            in_specs=[pl.BlockSpec((TOK_TILE, ROW_CHUNKS, LANES), lambda i, d, f, n: (i, 0, 0))],
            out_specs=pl.BlockSpec(memory_space=pl.ANY),
            scratch_shapes=[pltpu.VMEM((EXPERT_TILE, ROW_CHUNKS, LANES), F32), pltpu.SemaphoreType.DMA]),
        out_shape=jax.ShapeDtypeStruct((N_SORTED, ROW_CHUNKS, LANES), F32),
        compiler_params=_params("arbitrary"),
        name="moe_dispatch",
    )(dest_flat, fill_start, n_used, x1_tiles)


def _experts_kernel(layer, te_ref, nu_ref, first_ref, next_ref, slot_ref, x_ref, w1_hbm, b1_ref, w2_hbm, b2_ref,
                    o_ref, w1f_ref, w2f_ref, w1b_ref, w2b_ref, sem):
    i = pl.program_id(0)

    def weight_copies(e, slot):
        return (pltpu.make_async_copy(w1_hbm.at[layer, e], w1f_ref.at[slot], sem.at[0, slot]),
                pltpu.make_async_copy(w2_hbm.at[layer, e], w2f_ref.at[slot], sem.at[1, slot]))

    @pl.when(i == 0)
    def _():
        for c in weight_copies(te_ref[0], 0):
            c.start()

    @pl.when(first_ref[i] == 1)
    def _():
        slot = slot_ref[i]
        for c in weight_copies(te_ref[i], slot):
            c.wait()
        w1b_ref[...] = w1f_ref[slot].astype(BF16)
        w2b_ref[...] = w2f_ref[slot].astype(BF16)

        @pl.when(next_ref[i] >= 0)
        def _():
            for c in weight_copies(next_ref[i], 1 - slot):
                c.start()

    @pl.when(i < nu_ref[0])
    def _():
        x = _load_row_tiles(x_ref, EXPERT_TILE)
        h = jnp.dot(x.astype(BF16), w1b_ref[...], preferred_element_type=F32) + b1_ref[...]
        gate = jnp.minimum(h[:, :D_EXPERT], SWIGLU_LIMIT)
        up = jnp.clip(h[:, D_EXPERT:], -SWIGLU_LIMIT, SWIGLU_LIMIT)
        act = gate * jax.nn.sigmoid(SWIGLU_ALPHA * gate) * (up + 1.0)
        _store_row_tiles(o_ref, jnp.dot(act.astype(BF16), w2b_ref[...], preferred_element_type=F32) + b2_ref[...])

    @pl.when(i >= nu_ref[0])
    def _():
        o_ref[...] = jnp.zeros_like(o_ref)


def _experts(tables, xs, w1, b1, w2, b2, layer):
    bias = lambda n: pl.BlockSpec((None, None, 1, n), lambda i, te, *_: (layer, te[i], 0, 0))
    rows = pl.BlockSpec((EXPERT_TILE * ROW_CHUNKS, LANES), lambda i, te, nu, *_: (jnp.minimum(i, nu[0] - 1), 0))
    hbm = pl.BlockSpec(memory_space=pl.ANY)
    return pl.pallas_call(
        functools.partial(_experts_kernel, layer),
        grid_spec=pltpu.PrefetchScalarGridSpec(
            num_scalar_prefetch=len(tables),
            grid=(N_EXPERT_TILES,),
            in_specs=[rows, hbm, bias(2 * D_EXPERT), hbm, bias(D_MODEL)],
            out_specs=pl.BlockSpec((EXPERT_TILE * ROW_CHUNKS, LANES), lambda i, *_: (i, 0)),
            scratch_shapes=[pltpu.VMEM((2, D_MODEL, 2 * D_EXPERT), F32), pltpu.VMEM((2, D_EXPERT, D_MODEL), F32),
                            pltpu.VMEM((D_MODEL, 2 * D_EXPERT), BF16), pltpu.VMEM((D_EXPERT, D_MODEL), BF16),
                            pltpu.SemaphoreType.DMA((2, 2))]),
        out_shape=jax.ShapeDtypeStruct((N_SORTED * ROW_CHUNKS, LANES), F32),
        compiler_params=_params("arbitrary"),
        name="moe_experts",
    )(*tables, xs, w1, b1, w2, b2)


def _expert_tables(tile_expert, n_used, counts):
    tile = jnp.arange(N_EXPERT_TILES, dtype=jnp.int32)
    opens = (tile < n_used[0]) & ((tile == 0) | (tile_expert != jnp.roll(tile_expert, 1)))
    slot = (jnp.cumsum(opens.astype(jnp.int32)) - 1) % 2
    present = jnp.where(counts[0, :N_EXPERTS] > 0, jnp.arange(N_EXPERTS, dtype=jnp.int32), N_EXPERTS)
    after = jnp.concatenate([lax.cummin(present, reverse=True)[1:], jnp.full((1,), N_EXPERTS, jnp.int32)])
    is_expert = tile_expert[:, None] == jnp.arange(N_EXPERTS, dtype=jnp.int32)
    next_expert = jnp.sum(jnp.where(is_expert, jnp.where(after < N_EXPERTS, after, -1), 0), axis=1)
    return (tile_expert, n_used, opens.astype(jnp.int32), next_expert.astype(jnp.int32), slot.astype(jnp.int32))


def _combine_rows(dest_ref, x1_ref, gate_ref, g_ref, b_ref, ys_hbm, rows_ref, sem, tile_of):
    s = pl.program_id(0)
    last = N_TOK_TILES - 1

    def gather(tile, slot):
        base = tile * (TOK_TILE * TOP_K)
        for r in range(TOK_TILE):
            for k in range(TOP_K):
                pltpu.make_async_copy(ys_hbm.at[dest_ref[base + r * TOP_K + k]],
                                      rows_ref.at[slot, k, pl.ds(r * ROW_CHUNKS, ROW_CHUNKS), :],
                                      sem.at[slot]).start(priority=k % 2)

    def wait_rows(slot):
        for _ in range(TOP_K):
            pltpu.make_async_copy(ys_hbm.at[pl.ds(0, TOK_TILE)], ys_hbm.at[pl.ds(0, TOK_TILE)], sem.at[slot]).wait()

    @pl.when(s == 0)
    def _():
        gather(tile_of(0), 0)

    slot = s % 2
    wait_rows(slot)
    gather(tile_of(jnp.minimum(s + 1, last)), 1 - slot)
    gates = gate_ref[...]
    moe = gates[:, 0:1] * _load_row_tiles(rows_ref.at[slot, 0], TOK_TILE)
    for k in range(1, TOP_K):
        moe = moe + gates[:, k:k + 1] * _load_row_tiles(rows_ref.at[slot, k], TOK_TILE)
    out = _layer_norm(DEEPNORM_ALPHA * x1_ref[...] + moe, g_ref[...], b_ref[...])

    @pl.when(s == last)
    def _():
        wait_rows(1 - slot)

    return out


def _combine_kernel(dest_ref, x1_ref, gate_ref, g_ref, b_ref, ys_hbm, o_ref, rows_ref, sem):
    o_ref[...] = _combine_rows(dest_ref, x1_ref, gate_ref, g_ref, b_ref, ys_hbm, rows_ref, sem, lambda s: s)


def _combine_in_proj_kernel(dest_ref, x1_ref, gate_ref, g_ref, b_ref, ys_hbm, w_ref, o_ref, pool_ref, conv_ref,
                            qkvd_ref, q1_ref, q4_ref, q16_ref, kvt_ref, rows_ref, sem, wb_ref, stage_ref):
    _cast_w_in(w_ref, wb_ref)
    x = _combine_rows(dest_ref, x1_ref, gate_ref, g_ref, b_ref, ys_hbm, rows_ref, sem, _step_tile)
    o_ref[...] = x
    _in_proj_body(x, wb_ref, stage_ref, pool_ref, conv_ref, qkvd_ref, q1_ref, q4_ref, q16_ref, kvt_ref)


_COMBINE_SCRATCH = [pltpu.VMEM((2, TOP_K, TOK_TILE * ROW_CHUNKS, LANES), F32), pltpu.SemaphoreType.DMA((2,))]


def _combine(dest_flat, x1, gates, ln_g, ln_b, ys_tiles):
    row = pl.BlockSpec((TOK_TILE, D_MODEL), lambda i, d: (i, 0))
    vec = pl.BlockSpec((1, D_MODEL), lambda i, d: (0, 0))
    return pl.pallas_call(
        _combine_kernel,
        grid_spec=pltpu.PrefetchScalarGridSpec(
            num_scalar_prefetch=1,
            grid=(N_TOK_TILES,),
            in_specs=[row, pl.BlockSpec((TOK_TILE, LANES), lambda i, d: (i, 0)), vec, vec,
                      pl.BlockSpec(memory_space=pl.ANY)],
            out_specs=row,
            scratch_shapes=_COMBINE_SCRATCH),
        out_shape=jax.ShapeDtypeStruct((M_TOT, D_MODEL), F32),
        compiler_params=_params("arbitrary"),
        name="moe_combine",
    )(dest_flat, x1, gates, ln_g, ln_b, ys_tiles)


def _combine_in_proj(dest_flat, x1, gates, ln_g, ln_b, ys_tiles, w_in, next_layer):
    row = pl.BlockSpec((TOK_TILE, D_MODEL), lambda s, d: (_step_tile(s), 0))
    vec = pl.BlockSpec((1, D_MODEL), lambda s, d: (0, 0))
    specs, shapes, scratch = _in_proj_outputs()
    return pl.pallas_call(
        _combine_in_proj_kernel,
        grid_spec=pltpu.PrefetchScalarGridSpec(
            num_scalar_prefetch=1,
            grid=(N_TOK_TILES,),
            in_specs=[row, pl.BlockSpec((TOK_TILE, LANES), lambda s, d: (_step_tile(s), 0)), vec, vec,
                      pl.BlockSpec(memory_space=pl.ANY), _w_in_spec(next_layer)],
            out_specs=[row] + specs,
            scratch_shapes=_COMBINE_SCRATCH + scratch),
        out_shape=[jax.ShapeDtypeStruct((M_TOT, D_MODEL), F32)] + shapes,
        compiler_params=_params("arbitrary"),
        name="moe_combine_in_proj",
    )(dest_flat, x1, gates, ln_g, ln_b, ys_tiles, w_in)


def _routing_tables(info, counts):
    top_i = info[:, :TOP_K]
    rank = info[:, TOP_K:2 * TOP_K]
    cnt = counts[0, :N_EXPERTS].astype(jnp.int32)
    padded = (cnt + EXPERT_TILE - 1) // EXPERT_TILE * EXPERT_TILE
    pad_end = jnp.cumsum(padded)
    pad_start = pad_end - padded
    hot = top_i[:, :, None] == jnp.arange(N_EXPERTS, dtype=jnp.int32)
    dest = rank + jnp.sum(jnp.where(hot, pad_start, 0), axis=-1)
    fill_start = jnp.where(cnt % EXPERT_TILE != 0, pad_end - EXPERT_TILE, -1).astype(jnp.int32)
    tile_start = jnp.arange(N_EXPERT_TILES, dtype=jnp.int32) * EXPERT_TILE
    n_used = (pad_end[-1] // EXPERT_TILE).astype(jnp.int32)
    tile_expert = jnp.sum(tile_start[:, None] >= pad_end[None, :], axis=1).astype(jnp.int32)
    last_expert = tile_expert[jnp.maximum(n_used - 1, 0)]
    tile_expert = jnp.where(tile_start < pad_end[-1], tile_expert, last_expert)
    return dest.reshape(-1).astype(jnp.int32), fill_start, tile_expert, n_used.reshape(1)


def _block_diag(pool_w):
    g, n, _ = pool_w.shape
    out = jnp.zeros((g * n, g * n), pool_w.dtype)
    for i in range(g):
        out = out.at[i * n:(i + 1) * n, i * n:(i + 1) * n].set(pool_w[i])
    return out


def kernel(x_prompt, x_sample, state_pool, state_conv, cache_attn_k, cache_attn_v, w_in, pool_w, pool_scale, conv_w, conv_b, conv_ln_g, conv_ln_b, w_o, ln1_g, ln1_b, router_w, router_b, expert_w_in, expert_b_in, expert_w_out, expert_b_out, ln2_g, ln2_b):
    bias_prompt = jnp.asarray(_attn_bias_tables())
    bias_decode, mult_decode = (jnp.asarray(t) for t in _attn_decode_tables())
    tri = jnp.asarray(np.tril(np.ones((TOK_TILE, TOK_TILE), np.float32), -1), BF16)
    x = jnp.concatenate([x_prompt.reshape(M_PROMPT, D_MODEL), x_sample.reshape(DEC_BATCH, D_MODEL),
                         jnp.zeros((M_TOT - M_PROMPT - DEC_BATCH, D_MODEL), F32)], axis=0)
    cache_k_t = cache_attn_k.transpose(0, 1, 3, 4, 2)
    cache_v_t = cache_attn_v.transpose(0, 1, 3, 4, 2)
    b_in = expert_b_in.reshape(DEPTH, N_EXPERTS, 1, 2 * D_EXPERT)
    b_out = expert_b_out.reshape(DEPTH, N_EXPERTS, 1, D_MODEL)
    outs = {name: [] for name in ("p_pool", "p_conv", "p_k", "p_v", "s_pool", "s_conv", "s_k", "s_v")}
    proj = _in_proj(x, w_in, 0)
    for l in range(DEPTH):
        u_pool, c_in, qkv_d, q1, q4, q16, kv_t = proj
        w_bd = _block_diag(pool_w[l])
        scale = pool_scale[l].reshape(1, POOL_WIDTH)
        cb, cg, cbb = (v[l].reshape(1, CONV_WIDTH) for v in (conv_b, conv_ln_g, conv_ln_b))

        ya = _pool_prompt(u_pool, w_bd, scale)
        yc, glu = _conv_prompt(c_in, conv_w[l], cb, cg, cbb)
        branches = [_attn_prompt(q.reshape(-1, QKV_WIDTH), bias_prompt[bi], d)
                    for bi, (q, (_, d)) in enumerate(zip((q1, q4, q16), DILATED))]
        yb = _attn_merge(branches)

        sp_t = state_pool[l].transpose(1, 0, 2)
        sc_t = state_conv[l].transpose(1, 0, 2)
        ya_d, yc_d, glu_d = _mix_decode(u_pool, c_in, sp_t, sc_t, w_bd, scale, conv_w[l], cb, cg, cbb)
        qkv_d = qkv_d[:DEC_BATCH]
        yb_d = _attn_decode(qkv_d.reshape(DEC_BATCH, 3, ATTN_HEADS, HEAD_DIM, 1), cache_k_t, cache_v_t,
                            bias_decode, mult_decode, l)
        yb_d = jnp.pad(yb_d.reshape(DEC_BATCH, ATTN_WIDTH), ((0, TOK_TILE - DEC_BATCH), (0, 0)))

        rw = jnp.pad(router_w[l], ((0, 0), (0, LANES - N_EXPERTS)))
        rb = jnp.pad(router_b[l], (0, LANES - N_EXPERTS), constant_values=NEG_BIG).reshape(1, LANES)
        x1, x1_tiles, info, gates, counts = _post_mix(x, (ya, yb, yc), (ya_d, yb_d, yc_d), w_o,
                                                      ln1_g[l].reshape(1, D_MODEL), ln1_b[l].reshape(1, D_MODEL),
                                                      rw, rb, tri, l)

        dest, fill_start, tile_expert, n_used = _routing_tables(info, counts)
        xs = _dispatch(dest, fill_start, n_used, x1_tiles.reshape(M_TOT, ROW_CHUNKS, LANES))
        ys = _experts(_expert_tables(tile_expert, n_used, counts), xs.reshape(N_SORTED * ROW_CHUNKS, LANES),
                      expert_w_in, b_in, expert_w_out, b_out, l)
        combine_args = (dest, x1, gates, ln2_g[l].reshape(1, D_MODEL), ln2_b[l].reshape(1, D_MODEL),
                        ys.reshape(N_SORTED, ROW_CHUNKS, LANES))
        if l + 1 < DEPTH:
            x, *proj = _combine_in_proj(*combine_args, w_in, l + 1)
        else:
            x = _combine(*combine_args)

        kv_t = kv_t[:BATCH].reshape(BATCH, 2, ATTN_HEADS, HEAD_DIM, CACHE_LEN).transpose(1, 0, 4, 2, 3)
        last_rows = lambda a, n: a[:M_PROMPT].reshape(BATCH, SEQ, a.shape[-1])[:, SEQ - n:]
        outs["p_pool"].append(last_rows(u_pool, POOL_BUF))
        outs["p_conv"].append(last_rows(glu, CONV_BUF))
        outs["p_k"].append(kv_t[0])
        outs["p_v"].append(kv_t[1])
        u_d = u_pool[M_PROMPT:M_PROMPT + DEC_BATCH]
        outs["s_pool"].append(jnp.concatenate([state_pool[l][:, 1:], u_d[:, None]], axis=1))
        outs["s_conv"].append(jnp.concatenate([state_conv[l][:, 1:], glu_d[:, None]], axis=1))
        outs["s_k"].append(qkv_d[:, ATTN_WIDTH:2 * ATTN_WIDTH].reshape(DEC_BATCH, 1, ATTN_HEADS, HEAD_DIM))
        outs["s_v"].append(qkv_d[:, 2 * ATTN_WIDTH:].reshape(DEC_BATCH, 1, ATTN_HEADS, HEAD_DIM))

    y_p = x[:M_PROMPT].reshape(BATCH, SEQ, D_MODEL)
    y_s = x[M_PROMPT:M_PROMPT + DEC_BATCH].reshape(DEC_BATCH, 1, D_MODEL)
    return (y_p, y_s) + tuple(jnp.stack(outs[name]) for name in
                              ("p_pool", "p_conv", "p_k", "p_v", "s_pool", "s_conv", "s_k", "s_v"))
```

```python
import functools
import math

import numpy as np
import jax
import jax.numpy as jnp
from jax import lax
from jax.experimental import pallas as pl
from jax.experimental.pallas import tpu as pltpu

F32 = jnp.float32
BF16 = jnp.bfloat16

D_MODEL = 1024
BATCH = 4
SEQ = 4096
DEPTH = 4
DEC_BATCH = 32
PAST_LEN = 8192
POOL_WIDTH = 256
POOL_WINDOWS = (2, 4, 8, 16)
POOL_BUF = 15
ATTN_HEADS = 6
HEAD_DIM = 64
ATTN_WIDTH = ATTN_HEADS * HEAD_DIM
DILATED = ((128, 1), (512, 4), (2048, 16))
ATTN_BLOCK = 128
CACHE_LEN = 2048
CONV_WIDTH = 384
CONV_KERNEL = 31
CONV_BUF = 30
N_EXPERTS = 32
TOP_K = 4
D_EXPERT = 1024
SWIGLU_ALPHA = 1.702
SWIGLU_LIMIT = 7.0
DEEPNORM_ALPHA = (2 * DEPTH) ** 0.25
LN_EPS = 1e-5
NEG_BIG = -1e30
LANES = 128

M_PROMPT = BATCH * SEQ
TOK_TILE = 256
M_TOT = M_PROMPT + TOK_TILE
N_TOK_TILES = M_TOT // TOK_TILE
SEQ_TILE = 512
ATTN_Q_TILE = 2048
EXPERT_TILE = 512
N_ASSIGN = M_TOT * TOP_K
N_EXPERT_TILES = -(-N_ASSIGN // EXPERT_TILE) + N_EXPERTS
N_SORTED = N_EXPERT_TILES * EXPERT_TILE
VMEM_LIMIT = 56 * 1024 * 1024


def _alibi_slopes(n):
    def pow2(k):
        start = 2.0 ** (-8.0 / k)
        return [start ** (i + 1) for i in range(k)]
    if math.log2(n).is_integer():
        return pow2(n)
    c = 2 ** math.floor(math.log2(n))
    return pow2(c) + pow2(2 * c)[0::2][: n - c]


def _bdot(a, b):
    return jnp.dot(a.astype(BF16), b.astype(BF16), preferred_element_type=F32)


def _layer_norm(x, g, b):
    mu = jnp.mean(x, axis=-1, keepdims=True)
    xc = x - mu
    var = jnp.mean(xc * xc, axis=-1, keepdims=True)
    return xc * lax.rsqrt(var + LN_EPS) * g + b


def _params(*sem):
    return pltpu.CompilerParams(dimension_semantics=sem, vmem_limit_bytes=VMEM_LIMIT)


N_PROMPT_TILES = M_PROMPT // TOK_TILE
TILES_PER_SEQ = SEQ // TOK_TILE
QKV_WIDTH = 3 * ATTN_WIDTH
QKV_CHUNKS = QKV_WIDTH // LANES


def _cast_w_in(w_ref, wb_ref):
    @pl.when(pl.program_id(0) == 0)
    def _():
        wb_ref[...] = w_ref[...].astype(BF16)


def _in_proj_body(x, wb_ref, stage_ref, pool_ref, conv_ref, qkvd_ref, q1_ref, q4_ref, q16_ref, kvt_ref):
    res = jnp.dot(x.astype(BF16), wb_ref[...], preferred_element_type=F32)
    qkv = res[:, POOL_WIDTH:POOL_WIDTH + QKV_WIDTH]
    pool_ref[...] = res[:, :POOL_WIDTH]
    conv_ref[...] = res[:, POOL_WIDTH + QKV_WIDTH:]
    qkvd_ref[...] = qkv
    kvt_ref[...] = jnp.transpose(qkv[:, ATTN_WIDTH:])
    q1_ref[...] = qkv.astype(BF16)
    for c in range(QKV_CHUNKS):
        stage_ref[c] = qkv[:, c * LANES:(c + 1) * LANES]
    for ref, d in ((q4_ref, DILATED[1][1]), (q16_ref, DILATED[2][1])):
        n = TOK_TILE // d
        for r in range(d):
            rows = [stage_ref[c, pl.ds(r, n, stride=d), :] for c in range(QKV_CHUNKS)]
            ref[r] = jnp.concatenate(rows, axis=1).astype(BF16)


def _in_proj_kernel(x_ref, w_ref, pool_ref, conv_ref, qkvd_ref, q1_ref, q4_ref, q16_ref, kvt_ref, wb_ref, stage_ref):
    _cast_w_in(w_ref, wb_ref)
    _in_proj_body(x_ref[...], wb_ref, stage_ref, pool_ref, conv_ref, qkvd_ref, q1_ref, q4_ref, q16_ref, kvt_ref)


CACHE_TILE0 = (SEQ - CACHE_LEN) // TOK_TILE
N_IN = POOL_WIDTH + QKV_WIDTH + 2 * CONV_WIDTH


def _step_tile(s):
    return (s + N_TOK_TILES - 1) % N_TOK_TILES


def _in_proj_outputs():
    d4, d16 = DILATED[1][1], DILATED[2][1]
    ptile = lambda s: jnp.maximum(s - 1, 0)
    by_residue = lambda d: pl.BlockSpec(
        (None, d, TOK_TILE // d, QKV_WIDTH),
        lambda s, *_: (ptile(s) // TILES_PER_SEQ, 0, ptile(s) % TILES_PER_SEQ, 0))
    specs = [pl.BlockSpec((TOK_TILE, POOL_WIDTH), lambda s, *_: (_step_tile(s), 0)),
             pl.BlockSpec((TOK_TILE, 2 * CONV_WIDTH), lambda s, *_: (_step_tile(s), 0)),
             pl.BlockSpec((TOK_TILE, QKV_WIDTH), lambda s, *_: (jnp.minimum(s, 1), 0)),
             pl.BlockSpec((TOK_TILE, QKV_WIDTH), lambda s, *_: (_step_tile(s), 0)),
             by_residue(d4), by_residue(d16),
             pl.BlockSpec((None, 2 * ATTN_WIDTH, TOK_TILE),
                          lambda s, *_: (ptile(s) // TILES_PER_SEQ, 0,
                                         jnp.maximum(ptile(s) % TILES_PER_SEQ - CACHE_TILE0, 0)))]
    shapes = [jax.ShapeDtypeStruct((M_TOT, POOL_WIDTH), F32),
              jax.ShapeDtypeStruct((M_TOT, 2 * CONV_WIDTH), F32),
              jax.ShapeDtypeStruct((2 * TOK_TILE, QKV_WIDTH), F32),
              jax.ShapeDtypeStruct((M_TOT, QKV_WIDTH), BF16),
              jax.ShapeDtypeStruct((BATCH, d4, SEQ // d4, QKV_WIDTH), BF16),
              jax.ShapeDtypeStruct((BATCH, d16, SEQ // d16, QKV_WIDTH), BF16),
              jax.ShapeDtypeStruct((BATCH, 2 * ATTN_WIDTH, CACHE_LEN), F32)]
    scratch = [pltpu.VMEM((D_MODEL, N_IN), BF16), pltpu.VMEM((QKV_CHUNKS, TOK_TILE, LANES), F32)]
    return specs, shapes, scratch


def _w_in_spec(layer):
    return pl.BlockSpec((None, D_MODEL, N_IN), lambda s, *_: (layer, 0, 0), pipeline_mode=pl.Buffered(1))


def _in_proj(x, w_in, layer):
    specs, shapes, scratch = _in_proj_outputs()
    return pl.pallas_call(
        _in_proj_kernel,
        grid=(N_TOK_TILES,),
        in_specs=[pl.BlockSpec((TOK_TILE, D_MODEL), lambda s: (_step_tile(s), 0)), _w_in_spec(layer)],
        out_specs=specs,
        out_shape=shapes,
        scratch_shapes=scratch,
        compiler_params=_params("arbitrary"),
        name="in_proj",
    )(x, w_in)


def _pool_select(lane, vals):
    out = vals[-1]
    for g in range(len(vals) - 2, -1, -1):
        out = jnp.where(lane < (g + 1) * 64, vals[g], out)
    return out


def _pool_prompt_kernel(u_ref, w_ref, scale_ref, o_ref, halo_ref):
    t = pl.program_id(1)

    @pl.when(t == 0)
    def _():
        halo_ref[...] = jnp.zeros_like(halo_ref)

    u = u_ref[...]
    n = SEQ_TILE + 16
    ext = jnp.concatenate([halo_ref[...], u], axis=0)
    p2 = ext + pltpu.roll(ext, 1, 0)
    p4 = p2 + pltpu.roll(p2, 2, 0)
    p8 = p4 + pltpu.roll(p4, 4, 0)
    p16 = p8 + pltpu.roll(p8, 8, 0)
    lane = lax.broadcasted_iota(jnp.int32, (SEQ_TILE, POOL_WIDTH), 1)
    row = lax.broadcasted_iota(jnp.int32, (SEQ_TILE, POOL_WIDTH), 0)
    pos1 = (row + t * SEQ_TILE + 1).astype(F32)
    wsel = _pool_select(lane, [jnp.full((SEQ_TILE, POOL_WIDTH), float(w), F32) for w in POOL_WINDOWS])
    cnt = jnp.minimum(wsel, pos1)
    sums = _pool_select(lane, [p[16:n] for p in (p2, p4, p8, p16)])
    d = sums / cnt - u
    o_ref[...] = _bdot(d, w_ref[...]) * scale_ref[...]
    halo_ref[...] = u[SEQ_TILE - 16:]


def _pool_prompt(u_pool, w_bd, scale):
    nt = SEQ // SEQ_TILE
    return pl.pallas_call(
        _pool_prompt_kernel,
        grid=(BATCH, nt),
        in_specs=[pl.BlockSpec((SEQ_TILE, POOL_WIDTH), lambda b, t: (b * nt + t, 0)),
                  pl.BlockSpec((POOL_WIDTH, POOL_WIDTH), lambda b, t: (0, 0)),
                  pl.BlockSpec((1, POOL_WIDTH), lambda b, t: (0, 0))],
        out_specs=pl.BlockSpec((SEQ_TILE, POOL_WIDTH), lambda b, t: (b * nt + t, 0)),
        out_shape=jax.ShapeDtypeStruct((M_PROMPT, POOL_WIDTH), F32),
        scratch_shapes=[pltpu.VMEM((16, POOL_WIDTH), F32)],
        compiler_params=_params("arbitrary", "arbitrary"),
        name="pool_prompt",
    )(u_pool, w_bd, scale)


CONV_HALO = 32


def _conv_prompt_kernel(c_ref, w_ref, b_ref, g_ref, bb_ref, y_ref, u_ref, ext_ref, sh_ref):
    t = pl.program_id(1)

    @pl.when(t == 0)
    def _():
        ext_ref[0:CONV_HALO, :] = jnp.zeros((CONV_HALO, CONV_WIDTH), F32)

    c = c_ref[...]
    u = c[:, :CONV_WIDTH] * jax.nn.sigmoid(c[:, CONV_WIDTH:])
    u_ref[...] = u
    ext_ref[CONV_HALO:CONV_HALO + SEQ_TILE, :] = u
    span = SEQ_TILE + CONV_HALO - 8
    for ph in range(1, 8):
        sh_ref[ph, 0:span, :] = ext_ref[pl.ds(ph, span), :]

    acc = jnp.zeros((SEQ_TILE, CONV_WIDTH), F32)
    for j in range(CONV_KERNEL):
        off = j + 2
        a, ph = off // 8, off % 8
        tap = ext_ref[8 * a:8 * a + SEQ_TILE, :] if ph == 0 else sh_ref[ph, 8 * a:8 * a + SEQ_TILE, :]
        acc = acc + tap * w_ref[j:j + 1, :]
    y = _layer_norm(acc + b_ref[...], g_ref[...], bb_ref[...])
    y_ref[...] = y * jax.nn.sigmoid(y)
    ext_ref[0:CONV_HALO, :] = u[SEQ_TILE - CONV_HALO:]


def _conv_prompt(c_in, conv_w, conv_b, ln_g, ln_b):
    nt = SEQ // SEQ_TILE
    vec = pl.BlockSpec((1, CONV_WIDTH), lambda b, t: (0, 0))
    return pl.pallas_call(
        _conv_prompt_kernel,
        grid=(BATCH, nt),
        in_specs=[pl.BlockSpec((SEQ_TILE, 2 * CONV_WIDTH), lambda b, t: (b * nt + t, 0)),
                  pl.BlockSpec((CONV_KERNEL, CONV_WIDTH), lambda b, t: (0, 0)),
                  vec, vec, vec],
        out_specs=[pl.BlockSpec((SEQ_TILE, CONV_WIDTH), lambda b, t: (b * nt + t, 0)),
                   pl.BlockSpec((SEQ_TILE, CONV_WIDTH), lambda b, t: (b * nt + t, 0))],
        out_shape=[jax.ShapeDtypeStruct((M_PROMPT, CONV_WIDTH), F32),
                   jax.ShapeDtypeStruct((M_PROMPT, CONV_WIDTH), F32)],
        scratch_shapes=[pltpu.VMEM((SEQ_TILE + CONV_HALO, CONV_WIDTH), F32),
                        pltpu.VMEM((8, SEQ_TILE + CONV_HALO, CONV_WIDTH), F32)],
        compiler_params=_params("arbitrary", "arbitrary"),
        name="conv_prompt",
    )(c_in, conv_w, conv_b, ln_g, ln_b)


def _attn_bias_tables():
    slopes = _alibi_slopes(ATTN_HEADS)
    qi = np.arange(ATTN_BLOCK)[:, None] + ATTN_BLOCK
    ki = np.arange(2 * ATTN_BLOCK)[None, :]
    delta = qi - ki
    tabs = np.zeros((len(DILATED), ATTN_HEADS // 2, 4, ATTN_BLOCK, 2 * ATTN_BLOCK), np.float32)
    for bi, (w, d) in enumerate(DILATED):
        n_back = w // d
        band = (delta >= 0) & (delta <= n_back)
        for h in range(ATTN_HEADS):
            bias = -slopes[h] * (delta * d).astype(np.float32)
            for first in range(2):
                valid = band & (ki >= ATTN_BLOCK) if first else band
                tabs[bi, h // 2, 2 * (h % 2) + first] = np.where(valid, bias, NEG_BIG)
    return tabs


def _attn_prompt_kernel(blocks_per_seq, q_ref, kp_ref, kc_ref, vp_ref, vc_ref, bias_ref, o_ref, lse_ref):
    t = pl.program_id(1)
    kk = jnp.concatenate([kp_ref[...], kc_ref[...]], axis=0)
    vv = jnp.concatenate([vp_ref[...], vc_ref[...]], axis=0)
    lane_q = lax.broadcasted_iota(jnp.int32, (ATTN_BLOCK, 2 * HEAD_DIM), 1)
    for j in range(ATTN_Q_TILE // ATTN_BLOCK):
        blk = t * (ATTN_Q_TILE // ATTN_BLOCK) + j
        first = (blk % blocks_per_seq == 0).astype(jnp.int32)
        q = q_ref[j * ATTN_BLOCK:(j + 1) * ATTN_BLOCK, :]
        keys = kk[j * ATTN_BLOCK:(j + 2) * ATTN_BLOCK]
        vals = vv[j * ATTN_BLOCK:(j + 2) * ATTN_BLOCK]
        outs, lses = [], []
        for h in range(2):
            in_head = (lane_q >= h * HEAD_DIM) & (lane_q < (h + 1) * HEAD_DIM)
            qh = jnp.where(in_head, q, jnp.zeros_like(q))
            s = lax.dot_general(qh, keys, (((1,), (1,)), ((), ())), preferred_element_type=F32)
            s = s * (HEAD_DIM ** -0.5) + bias_ref[2 * h + first]
            m = jnp.max(s, axis=-1, keepdims=True)
            p = jnp.exp(s - m)
            l = jnp.sum(p, axis=-1, keepdims=True)
            acc = jnp.dot(p.astype(BF16), vals, preferred_element_type=F32)
            outs.append(acc / l)
            lses.append(m + jnp.log(l))
        o_ref[j * ATTN_BLOCK:(j + 1) * ATTN_BLOCK, :] = jnp.where(lane_q < HEAD_DIM, outs[0], outs[1])
        lse_ref[j * ATTN_BLOCK:(j + 1) * ATTN_BLOCK, :] = jnp.where(lane_q < HEAD_DIM, lses[0], lses[1])


def _attn_prompt(qkv, bias_tab, dilation):
    n_hp = ATTN_HEADS // 2
    per_tile = ATTN_Q_TILE // ATTN_BLOCK
    nt = M_PROMPT // ATTN_Q_TILE
    width = 2 * HEAD_DIM
    cur = lambda off: pl.BlockSpec((ATTN_Q_TILE, width), lambda hp, t: (t, off * n_hp + hp))
    prev = lambda off: pl.BlockSpec((ATTN_BLOCK, width),
                                    lambda hp, t: (jnp.maximum(t * per_tile - 1, 0), off * n_hp + hp))
    out = pl.BlockSpec((ATTN_Q_TILE, width), lambda hp, t: (t, hp))
    return pl.pallas_call(
        functools.partial(_attn_prompt_kernel, SEQ // dilation // ATTN_BLOCK),
        grid=(n_hp, nt),
        in_specs=[cur(0), prev(1), cur(1), prev(2), cur(2),
                  pl.BlockSpec((None, 4, ATTN_BLOCK, 2 * ATTN_BLOCK), lambda hp, t: (hp, 0, 0, 0))],
        out_specs=[out, out],
        out_shape=[jax.ShapeDtypeStruct((M_PROMPT, ATTN_WIDTH), F32),
                   jax.ShapeDtypeStruct((M_PROMPT, ATTN_WIDTH), F32)],
        compiler_params=_params("arbitrary", "arbitrary"),
        name=f"attn_prompt_d{dilation}",
    )(qkv, qkv, qkv, qkv, qkv, bias_tab)


ATTN_CHUNKS = ATTN_WIDTH // LANES


def _attn_merge_kernel(o1_ref, l1_ref, o4_ref, l4_ref, o16_ref, l16_ref, y_ref, so_ref, sl_ref):
    for j, (o_ref, l_ref, d) in enumerate(((o4_ref, l4_ref, DILATED[1][1]), (o16_ref, l16_ref, DILATED[2][1]))):
        n = TOK_TILE // d
        for r in range(d):
            ov, lv = o_ref[r], l_ref[r]
            for c in range(ATTN_CHUNKS):
                so_ref[j, c, pl.ds(r, n, stride=d), :] = ov[:, c * LANES:(c + 1) * LANES]
                sl_ref[j, c, pl.ds(r, n, stride=d), :] = lv[:, c * LANES:(c + 1) * LANES]
    outs = [o1_ref[...]] + [jnp.concatenate([so_ref[j, c] for c in range(ATTN_CHUNKS)], axis=1) for j in range(2)]
    lses = [l1_ref[...]] + [jnp.concatenate([sl_ref[j, c] for c in range(ATTN_CHUNKS)], axis=1) for j in range(2)]
    top = jnp.maximum(jnp.maximum(lses[0], lses[1]), lses[2])
    ws = [jnp.exp(v - top) for v in lses]
    y_ref[...] = (ws[0] * outs[0] + ws[1] * outs[1] + ws[2] * outs[2]) / (ws[0] + ws[1] + ws[2])


def _attn_merge(branches):
    (o1, l1), (o4, l4), (o16, l16) = branches
    d4, d16 = DILATED[1][1], DILATED[2][1]
    nat = pl.BlockSpec((TOK_TILE, ATTN_WIDTH), lambda i: (i, 0))
    by_residue = lambda d: pl.BlockSpec((None, d, TOK_TILE // d, ATTN_WIDTH),
                                        lambda i: (i // TILES_PER_SEQ, 0, i % TILES_PER_SEQ, 0))
    view = lambda a, d: a.reshape(BATCH, d, SEQ // d, ATTN_WIDTH)
    return pl.pallas_call(
        _attn_merge_kernel,
        grid=(N_PROMPT_TILES,),
        in_specs=[nat, nat, by_residue(d4), by_residue(d4), by_residue(d16), by_residue(d16)],
        out_specs=nat,
        out_shape=jax.ShapeDtypeStruct((M_PROMPT, ATTN_WIDTH), F32),
        scratch_shapes=[pltpu.VMEM((2, ATTN_CHUNKS, TOK_TILE, LANES), F32),
                        pltpu.VMEM((2, ATTN_CHUNKS, TOK_TILE, LANES), F32)],
        compiler_params=_params("arbitrary"),
        name="attn_merge",
    )(o1, l1, view(o4, d4), view(l4, d4), view(o16, d16), view(l16, d16))


def _mix_decode_kernel(u_ref, c_ref, sp_ref, sc_ref, pw_ref, ps_ref, cw_ref, cb_ref, cg_ref, cbb_ref,
                       ya_ref, yc_ref, glu_ref):
    nb = DEC_BATCH
    u = u_ref[0:nb, :]
    lane = lax.broadcasted_iota(jnp.int32, (nb, POOL_WIDTH), 1)
    run = u
    sums = []
    for back in range(1, max(POOL_WINDOWS)):
        run = run + sp_ref[POOL_BUF - back]
        if back + 1 in POOL_WINDOWS:
            sums.append(run)
    wsel = _pool_select(lane, [jnp.full((nb, POOL_WIDTH), float(w), F32) for w in POOL_WINDOWS])
    d = _pool_select(lane, sums) / wsel - u
    ya_ref[...] = jnp.zeros_like(ya_ref)
    ya_ref[0:nb, :] = _bdot(d, pw_ref[...]) * ps_ref[...]

    c = c_ref[0:nb, :]
    glu = c[:, :CONV_WIDTH] * jax.nn.sigmoid(c[:, CONV_WIDTH:])
    glu_ref[...] = glu
    acc = glu * cw_ref[CONV_KERNEL - 1:CONV_KERNEL, :]
    for j in range(CONV_BUF):
        acc = acc + sc_ref[j] * cw_ref[j:j + 1, :]
    y = _layer_norm(acc + cb_ref[...], cg_ref[...], cbb_ref[...])
    yc_ref[...] = jnp.zeros_like(yc_ref)
    yc_ref[0:nb, :] = y * jax.nn.sigmoid(y)


def _mix_decode(u_pool, c_in, sp_t, sc_t, w_bd, scale, conv_w, conv_b, ln_g, ln_b):
    last = N_TOK_TILES - 1
    full = lambda shape: pl.BlockSpec(shape, lambda i: (0,) * len(shape))
    return pl.pallas_call(
        _mix_decode_kernel,
        grid=(1,),
        in_specs=[pl.BlockSpec((TOK_TILE, POOL_WIDTH), lambda i: (last, 0)),
                  pl.BlockSpec((TOK_TILE, 2 * CONV_WIDTH), lambda i: (last, 0)),
                  full((POOL_BUF, DEC_BATCH, POOL_WIDTH)), full((CONV_BUF, DEC_BATCH, CONV_WIDTH)),
                  full((POOL_WIDTH, POOL_WIDTH)), full((1, POOL_WIDTH)),
                  full((CONV_KERNEL, CONV_WIDTH)), full((1, CONV_WIDTH)), full((1, CONV_WIDTH)),
                  full((1, CONV_WIDTH))],
        out_specs=[full((TOK_TILE, POOL_WIDTH)), full((TOK_TILE, CONV_WIDTH)), full((DEC_BATCH, CONV_WIDTH))],
        out_shape=[jax.ShapeDtypeStruct((TOK_TILE, POOL_WIDTH), F32),
                   jax.ShapeDtypeStruct((TOK_TILE, CONV_WIDTH), F32),
                   jax.ShapeDtypeStruct((DEC_BATCH, CONV_WIDTH), F32)],
        compiler_params=_params("arbitrary"),
        name="mix_decode",
    )(u_pool, c_in, sp_t, sc_t, w_bd, scale, conv_w, conv_b, ln_g, ln_b)


def _attn_decode_tables():
    slopes = _alibi_slopes(ATTN_HEADS)
    pos = np.arange(CACHE_LEN)
    dist = (CACHE_LEN - pos).astype(np.float32)
    bias = np.zeros((8, CACHE_LEN), np.float32)
    for h in range(ATTN_HEADS):
        bias[h] = -slopes[h] * dist
    mult = np.zeros((1, CACHE_LEN), np.float32)
    for w, d in DILATED:
        mult[0] += ((CACHE_LEN - pos) % d == 0) & (CACHE_LEN - pos <= w)
    return bias, mult


def _round_bf16(x):
    return x.astype(BF16).astype(F32)


def _attn_decode_kernel(qkv_ref, k_ref, v_ref, bias_ref, mult_ref, y_ref):
    scale = HEAD_DIM ** -0.5
    mult = mult_ref[...]
    n_branch = float(len(DILATED))
    mine = lax.broadcasted_iota(jnp.int32, (HEAD_DIM, DEC_BATCH), 1) == pl.program_id(0)

    def column(a):
        return jnp.sum(jnp.where(mine, a, 0.0), axis=1, keepdims=True)

    @pl.when(pl.program_id(0) == 0)
    def _():
        y_ref[...] = jnp.zeros_like(y_ref)

    for h in range(ATTN_HEADS):
        q = _round_bf16(column(qkv_ref[0, h]))
        k_new = _round_bf16(column(qkv_ref[1, h]))
        v_new = _round_bf16(column(qkv_ref[2, h]))
        s = jnp.sum(_round_bf16(k_ref[h]) * q, axis=0, keepdims=True) * scale + bias_ref[h:h + 1, :]
        s = jnp.where(mult > 0.0, s, NEG_BIG)
        s_self = jnp.sum(q * k_new, axis=0, keepdims=True) * scale
        m = jnp.maximum(jnp.max(s, axis=1, keepdims=True), s_self)
        p = mult * jnp.exp(s - m)
        p_self = n_branch * jnp.exp(s_self - m)
        l = jnp.sum(p, axis=1, keepdims=True) + p_self
        acc = jnp.sum(_round_bf16(v_ref[h]) * _round_bf16(p), axis=1, keepdims=True) + _round_bf16(p_self) * v_new
        y_ref[h] = jnp.where(mine, acc / l, y_ref[h])


def _attn_decode(qkv_cols, cache_k_t, cache_v_t, bias, mult, layer):
    cache = pl.BlockSpec((None, None, ATTN_HEADS, HEAD_DIM, CACHE_LEN), lambda b: (layer, b, 0, 0, 0))
    return pl.pallas_call(
        _attn_decode_kernel,
        grid=(DEC_BATCH,),
        in_specs=[pl.BlockSpec((3, ATTN_HEADS, HEAD_DIM, DEC_BATCH), lambda b: (0, 0, 0, 0)),
                  cache, cache,
                  pl.BlockSpec((8, CACHE_LEN), lambda b: (0, 0)),
                  pl.BlockSpec((1, CACHE_LEN), lambda b: (0, 0))],
        out_specs=pl.BlockSpec((ATTN_HEADS, HEAD_DIM, DEC_BATCH), lambda b: (0, 0, 0)),
        out_shape=jax.ShapeDtypeStruct((ATTN_HEADS, HEAD_DIM, DEC_BATCH), F32),
        compiler_params=_params("arbitrary"),
        name="attn_decode",
    )(qkv_cols, cache_k_t, cache_v_t, bias, mult)


ROW_CHUNKS = D_MODEL // LANES


def _store_row_tiles(ref, rows):
    n = rows.shape[0]
    for c in range(ROW_CHUNKS):
        ref[pl.ds(c, n, stride=ROW_CHUNKS), :] = rows[:, c * LANES:(c + 1) * LANES]


def _load_row_tiles(ref, n):
    return jnp.concatenate([ref[pl.ds(c, n, stride=ROW_CHUNKS), :] for c in range(ROW_CHUNKS)], axis=1)


def _post_mix_kernel(x_ref, ya_ref, yb_ref, yc_ref, yad_ref, ybd_ref, ycd_ref, wo_ref, g_ref, b_ref, rw_ref, rb_ref,
                     tri_ref, x1_ref, x1t_ref, info_ref, gate_ref, cnt_ref, wob_ref, run_ref):
    i = pl.program_id(0)

    @pl.when(i == 0)
    def _():
        wob_ref[...] = wo_ref[...].astype(BF16)
        run_ref[...] = jnp.zeros_like(run_ref)

    is_decode = i == N_TOK_TILES - 1
    mixed = jnp.concatenate([jnp.where(is_decode, d_ref[...], p_ref[...]) for p_ref, d_ref in
                             ((ya_ref, yad_ref), (yb_ref, ybd_ref), (yc_ref, ycd_ref))], axis=1)
    mix = jnp.dot(mixed.astype(BF16), wob_ref[...], preferred_element_type=F32)
    x1 = _layer_norm(DEEPNORM_ALPHA * x_ref[...] + mix, g_ref[...], b_ref[...])
    x1_ref[...] = x1
    _store_row_tiles(x1t_ref, x1)
    logits = _bdot(x1, rw_ref[...]) + rb_ref[...]
    lane = lax.broadcasted_iota(jnp.int32, (TOK_TILE, LANES), 1).astype(F32)
    work = logits
    chosen = jnp.zeros((TOK_TILE, LANES), F32)
    tops, idxs, hots = [], [], []
    for _ in range(TOP_K):
        mk = jnp.max(work, axis=-1, keepdims=True)
        idx = jnp.min(jnp.where(work == mk, lane, float(LANES)), axis=-1, keepdims=True)
        hot = lane == idx
        tops.append(mk)
        idxs.append(idx)
        hots.append(hot)
        chosen = chosen + jnp.where(hot, 1.0, 0.0)
        work = jnp.where(hot, -jnp.inf, work)
    exps = [jnp.exp(v - tops[0]) for v in tops]
    denom = exps[0] + exps[1] + exps[2] + exps[3]
    ahead = jnp.dot(tri_ref[...], chosen.astype(BF16), preferred_element_type=F32) + run_ref[0:1, :]
    ranks = [jnp.sum(jnp.where(hot, ahead, 0.0), axis=-1, keepdims=True) for hot in hots]
    info = jnp.zeros((TOK_TILE, LANES), F32)
    gates = jnp.zeros((TOK_TILE, LANES), F32)
    for k in range(TOP_K):
        info = jnp.where(lane == float(k), idxs[k], info)
        info = jnp.where(lane == float(TOP_K + k), ranks[k], info)
        gates = jnp.where(lane == float(k), exps[k] / denom, gates)
    info_ref[...] = info.astype(jnp.int32)
    gate_ref[...] = gates
    run_ref[...] = run_ref[...] + jnp.sum(chosen, axis=0, keepdims=True)
    cnt_ref[...] = run_ref[...]


def _post_mix(x, mix_prompt, mix_decode, w_o, ln_g, ln_b, router_w, router_b, tri, layer):
    row = pl.BlockSpec((TOK_TILE, D_MODEL), lambda i: (i, 0))
    vec = pl.BlockSpec((1, D_MODEL), lambda i: (0, 0))
    lanes = pl.BlockSpec((TOK_TILE, LANES), lambda i: (i, 0))
    widths = (POOL_WIDTH, ATTN_WIDTH, CONV_WIDTH)
    prompt = [pl.BlockSpec((TOK_TILE, w), lambda i: (jnp.minimum(i, N_PROMPT_TILES - 1), 0)) for w in widths]
    decode = [pl.BlockSpec((TOK_TILE, w), lambda i: (0, 0)) for w in widths]
    return pl.pallas_call(
        _post_mix_kernel,
        grid=(N_TOK_TILES,),
        in_specs=[row] + prompt + decode + [
                  pl.BlockSpec((None, D_MODEL, D_MODEL), lambda i: (layer, 0, 0), pipeline_mode=pl.Buffered(1)),
                  vec, vec,
                  pl.BlockSpec((D_MODEL, LANES), lambda i: (0, 0)),
                  pl.BlockSpec((1, LANES), lambda i: (0, 0)),
                  pl.BlockSpec((TOK_TILE, TOK_TILE), lambda i: (0, 0))],
        out_specs=[row, pl.BlockSpec((TOK_TILE * ROW_CHUNKS, LANES), lambda i: (i, 0)), lanes, lanes,
                   pl.BlockSpec((8, LANES), lambda i: (0, 0))],
        out_shape=[jax.ShapeDtypeStruct((M_TOT, D_MODEL), F32),
                   jax.ShapeDtypeStruct((M_TOT * ROW_CHUNKS, LANES), F32),
                   jax.ShapeDtypeStruct((M_TOT, LANES), jnp.int32),
                   jax.ShapeDtypeStruct((M_TOT, LANES), F32),
                   jax.ShapeDtypeStruct((8, LANES), F32)],
        scratch_shapes=[pltpu.VMEM((D_MODEL, D_MODEL), BF16), pltpu.VMEM((8, LANES), F32)],
        compiler_params=_params("arbitrary"),
        name="post_mix",
    )(x, *mix_prompt, *mix_decode, w_o, ln_g, ln_b, router_w, router_b, tri)


def _dispatch_kernel(dest_ref, fill_ref, nu_ref, x_ref, xs_hbm, zero_ref, sem):
    i = pl.program_id(0)

    @pl.when(i == 0)
    def _():
        zero_ref[...] = jnp.zeros_like(zero_ref)

        def tile_copy(start):
            return pltpu.make_async_copy(zero_ref, xs_hbm.at[pl.ds(start, EXPERT_TILE)], sem)

        def over_partial_tiles(act):
            def body(e, carry):
                @pl.when(fill_ref[e] >= 0)
                def _():
                    act(tile_copy(fill_ref[e]))
                return carry
            lax.fori_loop(0, N_EXPERTS, body, 0)

        def over_unused_tiles(act):
            def body(tile, carry):
                act(tile_copy(tile * EXPERT_TILE))
                return carry
            lax.fori_loop(nu_ref[0], N_EXPERT_TILES, body, 0)

        over_partial_tiles(lambda c: c.start())
        over_unused_tiles(lambda c: c.start())
        over_partial_tiles(lambda c: c.wait())
        over_unused_tiles(lambda c: c.wait())

    base = i * (TOK_TILE * TOP_K)
    for r in range(TOK_TILE):
        for k in range(TOP_K):
            pltpu.make_async_copy(x_ref.at[r], xs_hbm.at[dest_ref[base + r * TOP_K + k]], sem).start(priority=k % 2)
    for _ in range(TOP_K):
        pltpu.make_async_copy(x_ref, xs_hbm.at[pl.ds(0, TOK_TILE)], sem).wait()


def _dispatch(dest_flat, fill_start, n_used, x1_tiles):
    return pl.pallas_call(
        _dispatch_kernel,
        grid_spec=pltpu.PrefetchScalarGridSpec(
            num_scalar_prefetch=3,
            grid=(N_TOK_TILES,),
            in_specs=[pl.BlockSpec((TOK_TILE, ROW_CHUNKS, LANES), lambda i, d, f, n: (i, 0, 0))],
            out_specs=pl.BlockSpec(memory_space=pl.ANY),
            scratch_shapes=[pltpu.VMEM((EXPERT_TILE, ROW_CHUNKS, LANES), F32), pltpu.SemaphoreType.DMA]),
        out_shape=jax.ShapeDtypeStruct((N_SORTED, ROW_CHUNKS, LANES), F32),
        compiler_params=_params("arbitrary"),
        name="moe_dispatch",
    )(dest_flat, fill_start, n_used, x1_tiles)


def _experts_kernel(layer, te_ref, nu_ref, first_ref, next_ref, slot_ref, x_ref, w1_hbm, b1_ref, w2_hbm, b2_ref,
                    o_ref, w1f_ref, w2f_ref, w1b_ref, w2b_ref, sem):
    i = pl.program_id(0)

    def weight_copies(e, slot):
        return (pltpu.make_async_copy(w1_hbm.at[layer, e], w1f_ref.at[slot], sem.at[0, slot]),
                pltpu.make_async_copy(w2_hbm.at[layer, e], w2f_ref.at[slot], sem.at[1, slot]))

    @pl.when(i == 0)
    def _():
        for c in weight_copies(te_ref[0], 0):
            c.start()

    @pl.when(first_ref[i] == 1)
    def _():
        slot = slot_ref[i]
        for c in weight_copies(te_ref[i], slot):
            c.wait()
        w1b_ref[...] = w1f_ref[slot].astype(BF16)
        w2b_ref[...] = w2f_ref[slot].astype(BF16)

        @pl.when(next_ref[i] >= 0)
        def _():
            for c in weight_copies(next_ref[i], 1 - slot):
                c.start()

    @pl.when(i < nu_ref[0])
    def _():
        x = _load_row_tiles(x_ref, EXPERT_TILE)
        h = jnp.dot(x.astype(BF16), w1b_ref[...], preferred_element_type=F32) + b1_ref[...]
        gate = jnp.minimum(h[:, :D_EXPERT], SWIGLU_LIMIT)
        up = jnp.clip(h[:, D_EXPERT:], -SWIGLU_LIMIT, SWIGLU_LIMIT)
        act = gate * jax.nn.sigmoid(SWIGLU_ALPHA * gate) * (up + 1.0)
        _store_row_tiles(o_ref, jnp.dot(act.astype(BF16), w2b_ref[...], preferred_element_type=F32) + b2_ref[...])

    @pl.when(i >= nu_ref[0])
    def _():
        o_ref[...] = jnp.zeros_like(o_ref)


def _experts(tables, xs, w1, b1, w2, b2, layer):
    bias = lambda n: pl.BlockSpec((None, None, 1, n), lambda i, te, *_: (layer, te[i], 0, 0))
    rows = pl.BlockSpec((EXPERT_TILE * ROW_CHUNKS, LANES), lambda i, te, nu, *_: (jnp.minimum(i, nu[0] - 1), 0))
    hbm = pl.BlockSpec(memory_space=pl.ANY)
    return pl.pallas_call(
        functools.partial(_experts_kernel, layer),
        grid_spec=pltpu.PrefetchScalarGridSpec(
            num_scalar_prefetch=len(tables),
            grid=(N_EXPERT_TILES,),
            in_specs=[rows, hbm, bias(2 * D_EXPERT), hbm, bias(D_MODEL)],
            out_specs=pl.BlockSpec((EXPERT_TILE * ROW_CHUNKS, LANES), lambda i, *_: (i, 0)),
            scratch_shapes=[pltpu.VMEM((2, D_MODEL, 2 * D_EXPERT), F32), pltpu.VMEM((2, D_EXPERT, D_MODEL), F32),
                            pltpu.VMEM((D_MODEL, 2 * D_EXPERT), BF16), pltpu.VMEM((D_EXPERT, D_MODEL), BF16),
                            pltpu.SemaphoreType.DMA((2, 2))]),
        out_shape=jax.ShapeDtypeStruct((N_SORTED * ROW_CHUNKS, LANES), F32),
        compiler_params=_params("arbitrary"),
        name="moe_experts",
    )(*tables, xs, w1, b1, w2, b2)


def _expert_tables(tile_expert, n_used, counts):
    tile = jnp.arange(N_EXPERT_TILES, dtype=jnp.int32)
    opens = (tile < n_used[0]) & ((tile == 0) | (tile_expert != jnp.roll(tile_expert, 1)))
    slot = (jnp.cumsum(opens.astype(jnp.int32)) - 1) % 2
    present = jnp.where(counts[0, :N_EXPERTS] > 0, jnp.arange(N_EXPERTS, dtype=jnp.int32), N_EXPERTS)
    after = jnp.concatenate([lax.cummin(present, reverse=True)[1:], jnp.full((1,), N_EXPERTS, jnp.int32)])
    is_expert = tile_expert[:, None] == jnp.arange(N_EXPERTS, dtype=jnp.int32)
    next_expert = jnp.sum(jnp.where(is_expert, jnp.where(after < N_EXPERTS, after, -1), 0), axis=1)
    return (tile_expert, n_used, opens.astype(jnp.int32), next_expert.astype(jnp.int32), slot.astype(jnp.int32))


def _combine_rows(dest_ref, x1_ref, gate_ref, g_ref, b_ref, ys_hbm, rows_ref, sem, tile_of):
    s = pl.program_id(0)
    last = N_TOK_TILES - 1

    def gather(tile, slot):
        base = tile * (TOK_TILE * TOP_K)
        for r in range(TOK_TILE):
            for k in range(TOP_K):
                pltpu.make_async_copy(ys_hbm.at[dest_ref[base + r * TOP_K + k]],
                                      rows_ref.at[slot, k, pl.ds(r * ROW_CHUNKS, ROW_CHUNKS), :],
                                      sem.at[slot]).start(priority=k % 2)

    def wait_rows(slot):
        for _ in range(TOP_K):
            pltpu.make_async_copy(ys_hbm.at[pl.ds(0, TOK_TILE)], ys_hbm.at[pl.ds(0, TOK_TILE)], sem.at[slot]).wait()

    @pl.when(s == 0)
    def _():
        gather(tile_of(0), 0)

    slot = s % 2
    wait_rows(slot)
    gather(tile_of(jnp.minimum(s + 1, last)), 1 - slot)
    gates = gate_ref[...]
    moe = gates[:, 0:1] * _load_row_tiles(rows_ref.at[slot, 0], TOK_TILE)
    for k in range(1, TOP_K):
        moe = moe + gates[:, k:k + 1] * _load_row_tiles(rows_ref.at[slot, k], TOK_TILE)
    out = _layer_norm(DEEPNORM_ALPHA * x1_ref[...] + moe, g_ref[...], b_ref[...])

    @pl.when(s == last)
    def _():
        wait_rows(1 - slot)

    return out


def _combine_kernel(dest_ref, x1_ref, gate_ref, g_ref, b_ref, ys_hbm, o_ref, rows_ref, sem):
    o_ref[...] = _combine_rows(dest_ref, x1_ref, gate_ref, g_ref, b_ref, ys_hbm, rows_ref, sem, lambda s: s)


def _combine_in_proj_kernel(dest_ref, x1_ref, gate_ref, g_ref, b_ref, ys_hbm, w_ref, o_ref, pool_ref, conv_ref,
                            qkvd_ref, q1_ref, q4_ref, q16_ref, kvt_ref, rows_ref, sem, wb_ref, stage_ref):
    _cast_w_in(w_ref, wb_ref)
    x = _combine_rows(dest_ref, x1_ref, gate_ref, g_ref, b_ref, ys_hbm, rows_ref, sem, _step_tile)
    o_ref[...] = x
    _in_proj_body(x, wb_ref, stage_ref, pool_ref, conv_ref, qkvd_ref, q1_ref, q4_ref, q16_ref, kvt_ref)


_COMBINE_SCRATCH = [pltpu.VMEM((2, TOP_K, TOK_TILE * ROW_CHUNKS, LANES), F32), pltpu.SemaphoreType.DMA((2,))]


def _combine(dest_flat, x1, gates, ln_g, ln_b, ys_tiles):
    row = pl.BlockSpec((TOK_TILE, D_MODEL), lambda i, d: (i, 0))
    vec = pl.BlockSpec((1, D_MODEL), lambda i, d: (0, 0))
    return pl.pallas_call(
        _combine_kernel,
        grid_spec=pltpu.PrefetchScalarGridSpec(
            num_scalar_prefetch=1,
            grid=(N_TOK_TILES,),
            in_specs=[row, pl.BlockSpec((TOK_TILE, LANES), lambda i, d: (i, 0)), vec, vec,
                      pl.BlockSpec(memory_space=pl.ANY)],
            out_specs=row,
            scratch_shapes=_COMBINE_SCRATCH),
        out_shape=jax.ShapeDtypeStruct((M_TOT, D_MODEL), F32),
        compiler_params=_params("arbitrary"),
        name="moe_combine",
    )(dest_flat, x1, gates, ln_g, ln_b, ys_tiles)


def _combine_in_proj(dest_flat, x1, gates, ln_g, ln_b, ys_tiles, w_in, next_layer):
    row = pl.BlockSpec((TOK_TILE, D_MODEL), lambda s, d: (_step_tile(s), 0))
    vec = pl.BlockSpec((1, D_MODEL), lambda s, d: (0, 0))
    specs, shapes, scratch = _in_proj_outputs()
    return pl.pallas_call(
        _combine_in_proj_kernel,
        grid_spec=pltpu.PrefetchScalarGridSpec(
            num_scalar_prefetch=1,
            grid=(N_TOK_TILES,),
            in_specs=[row, pl.BlockSpec((TOK_TILE, LANES), lambda s, d: (_step_tile(s), 0)), vec, vec,
                      pl.BlockSpec(memory_space=pl.ANY), _w_in_spec(next_layer)],
            out_specs=[row] + specs,
            scratch_shapes=_COMBINE_SCRATCH + scratch),
        out_shape=[jax.ShapeDtypeStruct((M_TOT, D_MODEL), F32)] + shapes,
        compiler_params=_params("arbitrary"),
        name="moe_combine_in_proj",
    )(dest_flat, x1, gates, ln_g, ln_b, ys_tiles, w_in)


def _routing_tables(info, counts):
    top_i = info[:, :TOP_K]
    rank = info[:, TOP_K:2 * TOP_K]
    cnt = counts[0, :N_EXPERTS].astype(jnp.int32)
    padded = (cnt + EXPERT_TILE - 1) // EXPERT_TILE * EXPERT_TILE
    pad_end = jnp.cumsum(padded)
    pad_start = pad_end - padded
    hot = top_i[:, :, None] == jnp.arange(N_EXPERTS, dtype=jnp.int32)
    dest = rank + jnp.sum(jnp.where(hot, pad_start, 0), axis=-1)
    fill_start = jnp.where(cnt % EXPERT_TILE != 0, pad_end - EXPERT_TILE, -1).astype(jnp.int32)
    tile_start = jnp.arange(N_EXPERT_TILES, dtype=jnp.int32) * EXPERT_TILE
    n_used = (pad_end[-1] // EXPERT_TILE).astype(jnp.int32)
    tile_expert = jnp.sum(tile_start[:, None] >= pad_end[None, :], axis=1).astype(jnp.int32)
    last_expert = tile_expert[jnp.maximum(n_used - 1, 0)]
    tile_expert = jnp.where(tile_start < pad_end[-1], tile_expert, last_expert)
    return dest.reshape(-1).astype(jnp.int32), fill_start, tile_expert, n_used.reshape(1)


def _block_diag(pool_w):
    g, n, _ = pool_w.shape
    out = jnp.zeros((g * n, g * n), pool_w.dtype)
    for i in range(g):
        out = out.at[i * n:(i + 1) * n, i * n:(i + 1) * n].set(pool_w[i])
    return out


def kernel(x_prompt, x_sample, state_pool, state_conv, cache_attn_k, cache_attn_v, w_in, pool_w, pool_scale, conv_w, conv_b, conv_ln_g, conv_ln_b, w_o, ln1_g, ln1_b, router_w, router_b, expert_w_in, expert_b_in, expert_w_out, expert_b_out, ln2_g, ln2_b):
    bias_prompt = jnp.asarray(_attn_bias_tables())
    bias_decode, mult_decode = (jnp.asarray(t) for t in _attn_decode_tables())
    tri = jnp.asarray(np.tril(np.ones((TOK_TILE, TOK_TILE), np.float32), -1), BF16)
    x = jnp.concatenate([x_prompt.reshape(M_PROMPT, D_MODEL), x_sample.reshape(DEC_BATCH, D_MODEL),
                         jnp.zeros((M_TOT - M_PROMPT - DEC_BATCH, D_MODEL), F32)], axis=0)
    cache_k_t = cache_attn_k.transpose(0, 1, 3, 4, 2)
    cache_v_t = cache_attn_v.transpose(0, 1, 3, 4, 2)
    b_in = expert_b_in.reshape(DEPTH, N_EXPERTS, 1, 2 * D_EXPERT)
    b_out = expert_b_out.reshape(DEPTH, N_EXPERTS, 1, D_MODEL)
    outs = {name: [] for name in ("p_pool", "p_conv", "p_k", "p_v", "s_pool", "s_conv", "s_k", "s_v")}
    proj = _in_proj(x, w_in, 0)
    for l in range(DEPTH):
        u_pool, c_in, qkv_d, q1, q4, q16, kv_t = proj
        w_bd = _block_diag(pool_w[l])
        scale = pool_scale[l].reshape(1, POOL_WIDTH)
        cb, cg, cbb = (v[l].reshape(1, CONV_WIDTH) for v in (conv_b, conv_ln_g, conv_ln_b))

        ya = _pool_prompt(u_pool, w_bd, scale)
        yc, glu = _conv_prompt(c_in, conv_w[l], cb, cg, cbb)
        branches = [_attn_prompt(q.reshape(-1, QKV_WIDTH), bias_prompt[bi], d)
                    for bi, (q, (_, d)) in enumerate(zip((q1, q4, q16), DILATED))]
        yb = _attn_merge(branches)

        sp_t = state_pool[l].transpose(1, 0, 2)
        sc_t = state_conv[l].transpose(1, 0, 2)
        ya_d, yc_d, glu_d = _mix_decode(u_pool, c_in, sp_t, sc_t, w_bd, scale, conv_w[l], cb, cg, cbb)
        qkv_d = qkv_d[:DEC_BATCH]
        yb_d = _attn_decode(qkv_d.T.reshape(3, ATTN_HEADS, HEAD_DIM, DEC_BATCH), cache_k_t, cache_v_t,
                            bias_decode, mult_decode, l)
        yb_d = jnp.pad(yb_d.reshape(ATTN_WIDTH, DEC_BATCH).T, ((0, TOK_TILE - DEC_BATCH), (0, 0)))

        rw = jnp.pad(router_w[l], ((0, 0), (0, LANES - N_EXPERTS)))
        rb = jnp.pad(router_b[l], (0, LANES - N_EXPERTS), constant_values=NEG_BIG).reshape(1, LANES)
        x1, x1_tiles, info, gates, counts = _post_mix(x, (ya, yb, yc), (ya_d, yb_d, yc_d), w_o,
                                                      ln1_g[l].reshape(1, D_MODEL), ln1_b[l].reshape(1, D_MODEL),
                                                      rw, rb, tri, l)

        dest, fill_start, tile_expert, n_used = _routing_tables(info, counts)
        xs = _dispatch(dest, fill_start, n_used, x1_tiles.reshape(M_TOT, ROW_CHUNKS, LANES))
        ys = _experts(_expert_tables(tile_expert, n_used, counts), xs.reshape(N_SORTED * ROW_CHUNKS, LANES),
                      expert_w_in, b_in, expert_w_out, b_out, l)
        combine_args = (dest, x1, gates, ln2_g[l].reshape(1, D_MODEL), ln2_b[l].reshape(1, D_MODEL),
                        ys.reshape(N_SORTED, ROW_CHUNKS, LANES))
        if l + 1 < DEPTH:
            x, *proj = _combine_in_proj(*combine_args, w_in, l + 1)
        else:
            x = _combine(*combine_args)

        kv_t = kv_t[:BATCH].reshape(BATCH, 2, ATTN_HEADS, HEAD_DIM, CACHE_LEN).transpose(1, 0, 4, 2, 3)
        last_rows = lambda a, n: a[:M_PROMPT].reshape(BATCH, SEQ, a.shape[-1])[:, SEQ - n:]
        outs["p_pool"].append(last_rows(u_pool, POOL_BUF))
        outs["p_conv"].append(last_rows(glu, CONV_BUF))
        outs["p_k"].append(kv_t[0])
        outs["p_v"].append(kv_t[1])
        u_d = u_pool[M_PROMPT:M_PROMPT + DEC_BATCH]
        outs["s_pool"].append(jnp.concatenate([state_pool[l][:, 1:], u_d[:, None]], axis=1))
        outs["s_conv"].append(jnp.concatenate([state_conv[l][:, 1:], glu_d[:, None]], axis=1))
        outs["s_k"].append(qkv_d[:, ATTN_WIDTH:2 * ATTN_WIDTH].reshape(DEC_BATCH, 1, ATTN_HEADS, HEAD_DIM))
        outs["s_v"].append(qkv_d[:, 2 * ATTN_WIDTH:].reshape(DEC_BATCH, 1, ATTN_HEADS, HEAD_DIM))

    y_p = x[:M_PROMPT].reshape(BATCH, SEQ, D_MODEL)
    y_s = x[M_PROMPT:M_PROMPT + DEC_BATCH].reshape(DEC_BATCH, 1, D_MODEL)
    return (y_p, y_s) + tuple(jnp.stack(outs[name]) for name in
                              ("p_pool", "p_conv", "p_k", "p_v", "s_pool", "s_conv", "s_k", "s_v"))
```
